```python
import math
import jax, jax.numpy as jnp
from jax import lax
import numpy as np

D_MODEL = 1024
BATCH = 8
SEQ = 4096
DEPTH = 2

CHUNK = 64
EPS = 1e-6
D_FF = 4 * D_MODEL

SSD_HEADS = 16
SSD_HEAD_DIM = 64
SSD_INNER = SSD_HEADS * SSD_HEAD_DIM
SSD_GROUPS = 2
SSD_STATE = 128
SSD_CONV = 4
SSD_CONV_DIM = SSD_INNER + 2 * SSD_GROUPS * SSD_STATE
DT_MIN = 0.001
DT_MAX = 0.1

SWA_HEADS = 16
SWA_KV_HEADS = 4
SWA_HEAD_DIM = 64
SWA_WINDOW = 128
SWA_BLOCK = 128
SWA_WINDOW_CHUNKS = SWA_WINDOW // CHUNK

EVEN_IN = SSD_INNER + SSD_CONV_DIM + SSD_HEADS + (SWA_HEADS + 2 * SWA_KV_HEADS) * SWA_HEAD_DIM
EVEN_MIX = SSD_INNER + SWA_HEADS * SWA_HEAD_DIM

DIFF_HEADS = 8
DIFF_HEAD_DIM = 64
DIFF_MIX = 2 * DIFF_HEADS * DIFF_HEAD_DIM
DIFF_IN = 3 * DIFF_MIX
Q_BLOCK = 128

N_EVEN = (DEPTH + 1) // 2
N_ODD = DEPTH // 2

kernel_name = 'hybrid_ssd_swa_diffattn_trunk'


def rms_norm(x, w):
    xf = x.astype(jnp.float32)
    y = xf * lax.rsqrt(jnp.mean(xf * xf, axis=-1, keepdims=True) + EPS)
    return (y * w.astype(jnp.float32)).astype(x.dtype)


def alibi_slopes(n):
    return 2.0 ** (-8.0 * jnp.arange(1, n + 1, dtype=jnp.float32) / n)


def causal_depthwise_conv(x, w, b):
    k = w.shape[0]
    y = lax.conv_general_dilated(x, w[:, None, :].astype(x.dtype), window_strides=(1,),
                                 padding=((k - 1, 0),), dimension_numbers=('NWC', 'WIO', 'NWC'),
                                 feature_group_count=x.shape[-1])
    return y + b


def ssd_scan(x, dt, a, b_mat, c_mat):
    bsz, t, h, p = x.shape
    g, n = b_mat.shape[-2:]
    e = h // g
    nc = t // CHUNK
    f32 = jnp.float32
    xdt = (x.astype(f32) * dt[..., None]).reshape(bsz, nc, CHUNK, g, e, p)
    bm = b_mat.astype(f32).reshape(bsz, nc, CHUNK, g, n)
    cm = c_mat.astype(f32).reshape(bsz, nc, CHUNK, g, n)
    a_dt = (dt * a).reshape(bsz, nc, CHUNK, g, e).transpose(0, 3, 4, 1, 2)
    a_cum = jnp.cumsum(a_dt, axis=-1)
    seg = a_cum[..., :, None] - a_cum[..., None, :]
    causal = jnp.tril(jnp.ones((CHUNK, CHUNK), dtype=bool))
    decay = jnp.exp(jnp.where(causal, seg, -jnp.inf))
    cb = jnp.einsum('bclgn,bcsgn->bgcls', cm, bm)
    y_diag = jnp.einsum('bgecls,bcsgep->bclgep', cb[:, :, None] * decay, xdt)
    decay_to_end = jnp.exp(a_cum[..., -1:] - a_cum).transpose(0, 3, 4, 1, 2)
    chunk_states = jnp.einsum('bclgn,bclgep->cbgepn', bm, xdt * decay_to_end[..., None])
    chunk_decay = jnp.exp(a_cum[..., -1]).transpose(3, 0, 1, 2)

    def step(state, inp):
        dec, new = inp
        return state * dec[..., None, None] + new, state

    init = jnp.zeros((bsz, g, e, p, n), f32)
    _, prev = lax.scan(step, init, (chunk_decay, chunk_states))
    decay_in = jnp.exp(a_cum).transpose(0, 3, 4, 1, 2)
    y_off = jnp.einsum('bclgn,cbgepn->bclgep', cm, prev) * decay_in[..., None]
    return (y_diag + y_off).reshape(bsz, t, h, p)


def swa_attention(q, k, v, q_norm, k_norm, sinks):
    bsz, t = q.shape[:2]
    nb = t // SWA_BLOCK
    r = SWA_HEADS // SWA_KV_HEADS
    f32 = jnp.float32
    q = rms_norm(q, q_norm).reshape(bsz, nb, SWA_BLOCK, SWA_KV_HEADS, r, SWA_HEAD_DIM)
    k = rms_norm(k, k_norm)

    def band(z):
        zb = z.reshape(bsz, nb, SWA_BLOCK, SWA_KV_HEADS, SWA_HEAD_DIM)
        prev = jnp.pad(zb[:, :-1], ((0, 0), (1, 0), (0, 0), (0, 0), (0, 0)))
        return jnp.concatenate([prev, zb], axis=2)

    kb, vb = band(k), band(v)
    s = jnp.einsum('bnqhrd,bnkhd->bhrnqk', q, kb, preferred_element_type=f32) * (SWA_HEAD_DIM ** -0.5)
    qpos = jnp.arange(t).reshape(nb, SWA_BLOCK)
    kpos = qpos[:, :1] - SWA_BLOCK + jnp.arange(2 * SWA_BLOCK)
    dchunk = qpos[:, :, None] // CHUNK - kpos[:, None, :] // CHUNK
    valid = (kpos[:, None, :] >= 0) & (dchunk >= 0) & (dchunk <= SWA_WINDOW_CHUNKS)
    dist = jnp.abs(qpos[:, :, None] - kpos[:, None, :]).astype(f32)
    slopes = alibi_slopes(SWA_HEADS).reshape(1, SWA_KV_HEADS, r, 1, 1, 1)
    s = jnp.where(valid, s - slopes * dist, -jnp.inf)
    sink = sinks.astype(f32).reshape(1, SWA_KV_HEADS, r, 1, 1)
    m = jnp.maximum(jnp.max(s, axis=-1), sink)
    ex = jnp.exp(s - m[..., None])
    prob = ex / (jnp.sum(ex, axis=-1, keepdims=True) + jnp.exp(sink - m)[..., None])
    o = jnp.einsum('bhrnqk,bnkhd->bnqhrd', prob.astype(v.dtype), vb)
    return o.reshape(bsz, t, SWA_HEADS * SWA_HEAD_DIM)


def even_mixer(h, w_in, conv_w, conv_b, dt_bias, a_log, d_skip, ssd_norm_w, q_norm, k_norm, sinks, w_out):
    bsz, t, _ = h.shape
    proj = h @ w_in
    cuts = [SSD_INNER, SSD_INNER + SSD_CONV_DIM, SSD_INNER + SSD_CONV_DIM + SSD_HEADS,
            SSD_INNER + SSD_CONV_DIM + SSD_HEADS + SWA_HEADS * SWA_HEAD_DIM,
            SSD_INNER + SSD_CONV_DIM + SSD_HEADS + (SWA_HEADS + SWA_KV_HEADS) * SWA_HEAD_DIM]
    z, xbc, dt_raw, q, k, v = jnp.split(proj, cuts, axis=-1)
    xbc = jax.nn.silu(causal_depthwise_conv(xbc, conv_w, conv_b))
    xs, bm, cm = jnp.split(xbc, [SSD_INNER, SSD_INNER + SSD_GROUPS * SSD_STATE], axis=-1)
    xs = xs.reshape(bsz, t, SSD_HEADS, SSD_HEAD_DIM)
    dt = jax.nn.softplus(dt_raw.astype(jnp.float32) + dt_bias.astype(jnp.float32))
    a = -jnp.exp(a_log.astype(jnp.float32))
    y = ssd_scan(xs, dt, a, bm.reshape(bsz, t, SSD_GROUPS, SSD_STATE), cm.reshape(bsz, t, SSD_GROUPS, SSD_STATE))
    y = y + xs.astype(jnp.float32) * d_skip.astype(jnp.float32)[:, None]
    y = y.reshape(bsz, t, SSD_INNER) * jax.nn.silu(z.astype(jnp.float32))
    y = rms_norm(y.reshape(bsz, t, SSD_GROUPS, SSD_INNER // SSD_GROUPS), ssd_norm_w.reshape(SSD_GROUPS, -1))
    y_ssd = y.reshape(bsz, t, SSD_INNER).astype(h.dtype)
    y_swa = swa_attention(q.reshape(bsz, t, SWA_HEADS, SWA_HEAD_DIM),
                          k.reshape(bsz, t, SWA_KV_HEADS, SWA_HEAD_DIM),
                          v.reshape(bsz, t, SWA_KV_HEADS, SWA_HEAD_DIM), q_norm, k_norm, sinks).astype(h.dtype)
    return jnp.concatenate([y_ssd, y_swa], axis=-1) @ w_out


def diff_attention(h, w_in, q_norm, k_norm, lam_q1, lam_k1, lam_q2, lam_k2, sub_norm, w_out, layer_idx):
    bsz, t, _ = h.shape
    f32 = jnp.float32
    q, k, v = jnp.split(h @ w_in, 3, axis=-1)
    q = rms_norm(q.reshape(bsz, t, DIFF_HEADS, 2, DIFF_HEAD_DIM), q_norm)
    k = rms_norm(k.reshape(bsz, t, DIFF_HEADS, 2, DIFF_HEAD_DIM), k_norm)
    v = v.reshape(bsz, t, DIFF_HEADS, 2 * DIFF_HEAD_DIM)
    lam_init = 0.8 - 0.6 * math.exp(-0.3 * layer_idx)
    lam = (jnp.exp(jnp.sum(lam_q1.astype(f32) * lam_k1.astype(f32)))
           - jnp.exp(jnp.sum(lam_q2.astype(f32) * lam_k2.astype(f32))) + lam_init)
    slopes = alibi_slopes(DIFF_HEADS)[None, :, None, None, None]
    scale = DIFF_HEAD_DIM ** -0.5
    outs = []
    for i in range(t // Q_BLOCK):
        q0 = i * Q_BLOCK
        kend = q0 + Q_BLOCK
        s = jnp.einsum('bqhmd,bkhmd->bhmqk', q[:, q0:kend], k[:, :kend], preferred_element_type=f32) * scale
        qpos = q0 + jnp.arange(Q_BLOCK)
        kpos = jnp.arange(kend)
        dist = jnp.abs(qpos[:, None] - kpos[None, :]).astype(f32)
        mask = (kpos[None, :] // CHUNK) <= (qpos[:, None] // CHUNK)
        s = jnp.where(mask, s - slopes * dist, -jnp.inf)
        prob = jax.nn.softmax(s, axis=-1)
        attn = prob[:, :, 0] - lam * prob[:, :, 1]
        outs.append(jnp.einsum('bhqk,bkhd->bqhd', attn.astype(v.dtype), v[:, :kend]))
    o = jnp.concatenate(outs, axis=1)
    o = (rms_norm(o, sub_norm).astype(f32) * (1.0 - lam_init)).astype(h.dtype)
    return o.reshape(bsz, t, DIFF_MIX) @ w_out


def setup_inputs(seed: int = 0) -> dict:
    key = jax.random.key(seed)
    ks = iter(jax.random.split(key, 32))

    def nrm(shape, scale):
        return scale * jax.random.normal(next(ks), shape, jnp.float32)

    x = nrm((BATCH, SEQ, D_MODEL), 1.0)
    ev_norm_w = 1.0 + nrm((N_EVEN, D_MODEL), 0.02)
    ev_w_in = nrm((N_EVEN, D_MODEL, EVEN_IN), D_MODEL ** -0.5)
    ev_conv_w = nrm((N_EVEN, SSD_CONV, SSD_CONV_DIM), SSD_CONV ** -0.5)
    ev_conv_b = nrm((N_EVEN, SSD_CONV_DIM), 0.02)
    u = jax.random.uniform(next(ks), (N_EVEN, SSD_HEADS), jnp.float32)
    dt0 = jnp.maximum(jnp.exp(u * (math.log(DT_MAX) - math.log(DT_MIN)) + math.log(DT_MIN)), 1e-4)
    ev_dt_bias = dt0 + jnp.log(-jnp.expm1(-dt0))
    ev_a_log = jnp.log(jax.random.uniform(next(ks), (N_EVEN, SSD_HEADS), jnp.float32, 1.0, 16.0))
    ev_d_skip = 1.0 + nrm((N_EVEN, SSD_HEADS), 0.02)
    ev_ssd_norm_w = 1.0 + nrm((N_EVEN, SSD_INNER), 0.02)
    ev_q_norm = 1.0 + nrm((N_EVEN, SWA_HEAD_DIM), 0.02)
    ev_k_norm = 1.0 + nrm((N_EVEN, SWA_HEAD_DIM), 0.02)
    ev_sinks = nrm((N_EVEN, SWA_HEADS), 0.5)
    ev_w_out = nrm((N_EVEN, EVEN_MIX, D_MODEL), EVEN_MIX ** -0.5)
    od_norm_w = 1.0 + nrm((N_ODD, D_MODEL), 0.02)
    od_w_in = nrm((N_ODD, D_MODEL, DIFF_IN), D_MODEL ** -0.5)
    od_q_norm = 1.0 + nrm((N_ODD, DIFF_HEAD_DIM), 0.02)
    od_k_norm = 1.0 + nrm((N_ODD, DIFF_HEAD_DIM), 0.02)
    od_lam_q1 = nrm((N_ODD, DIFF_HEAD_DIM), 0.1)
    od_lam_k1 = nrm((N_ODD, DIFF_HEAD_DIM), 0.1)
    od_lam_q2 = nrm((N_ODD, DIFF_HEAD_DIM), 0.1)
    od_lam_k2 = nrm((N_ODD, DIFF_HEAD_DIM), 0.1)
    od_sub_norm = 1.0 + nrm((N_ODD, 2 * DIFF_HEAD_DIM), 0.02)
    od_w_out = nrm((N_ODD, DIFF_MIX, D_MODEL), DIFF_MIX ** -0.5)
    mlp_norm_w = 1.0 + nrm((DEPTH, D_MODEL), 0.02)
    mlp_w1 = nrm((DEPTH, D_MODEL, D_FF), D_MODEL ** -0.5)
    mlp_w2 = nrm((DEPTH, D_FF, D_MODEL), D_FF ** -0.5)
    return {'x': x, 'ev_norm_w': ev_norm_w, 'ev_w_in': ev_w_in, 'ev_conv_w': ev_conv_w, 'ev_conv_b': ev_conv_b,
            'ev_dt_bias': ev_dt_bias, 'ev_a_log': ev_a_log, 'ev_d_skip': ev_d_skip, 'ev_ssd_norm_w': ev_ssd_norm_w,
            'ev_q_norm': ev_q_norm, 'ev_k_norm': ev_k_norm, 'ev_sinks': ev_sinks, 'ev_w_out': ev_w_out,
            'od_norm_w': od_norm_w, 'od_w_in': od_w_in, 'od_q_norm': od_q_norm, 'od_k_norm': od_k_norm,
            'od_lam_q1': od_lam_q1, 'od_lam_k1': od_lam_k1, 'od_lam_q2': od_lam_q2, 'od_lam_k2': od_lam_k2,
            'od_sub_norm': od_sub_norm, 'od_w_out': od_w_out,
            'mlp_norm_w': mlp_norm_w, 'mlp_w1': mlp_w1, 'mlp_w2': mlp_w2}


def reference(x, ev_norm_w, ev_w_in, ev_conv_w, ev_conv_b, ev_dt_bias, ev_a_log, ev_d_skip, ev_ssd_norm_w,
              ev_q_norm, ev_k_norm, ev_sinks, ev_w_out,
              od_norm_w, od_w_in, od_q_norm, od_k_norm, od_lam_q1, od_lam_k1, od_lam_q2, od_lam_k2,
              od_sub_norm, od_w_out, mlp_norm_w, mlp_w1, mlp_w2):
    for layer in range(DEPTH):
        j = layer // 2
        if layer % 2 == 0:
            mix = even_mixer(rms_norm(x, ev_norm_w[j]), ev_w_in[j], ev_conv_w[j], ev_conv_b[j], ev_dt_bias[j],
                             ev_a_log[j], ev_d_skip[j], ev_ssd_norm_w[j], ev_q_norm[j], ev_k_norm[j],
                             ev_sinks[j], ev_w_out[j])
        else:
            mix = diff_attention(rms_norm(x, od_norm_w[j]), od_w_in[j], od_q_norm[j], od_k_norm[j],
                                 od_lam_q1[j], od_lam_k1[j], od_lam_q2[j], od_lam_k2[j],
                                 od_sub_norm[j], od_w_out[j], layer)
        x = x + mix.astype(x.dtype)
        hid = rms_norm(x, mlp_norm_w[layer]) @ mlp_w1[layer]
        x = x + (jnp.square(jax.nn.relu(hid)) @ mlp_w2[layer]).astype(x.dtype)
    return x
```

```python
import functools
import math

import jax
import jax.numpy as jnp
from jax import lax
from jax.experimental import pallas as pl
from jax.experimental.pallas import tpu as pltpu

F32 = jnp.float32
BF16 = jnp.bfloat16

LANES = 128
V7X_VMEM_LIMIT_BYTES = 56 * 1024 * 1024

EPS = 1e-6
CHUNK = 64
HEAD_DIM = 64
GROUP_TILE = 512
SSD_HEADS = 16
SSD_STATE = 128
SSD_GROUPS = 2
SSD_INNER = SSD_HEADS * HEAD_DIM
SSD_CONV = 4
SWA_HEADS = 16
SWA_KV_HEADS = 4
SWA_BLOCK = 128
DIFF_HEADS = 8
DIFF_BLOCK = 512
SSD_TILE = 256

NT_DIMS = (((1,), (1,)), ((), ()))
TN_DIMS = (((0,), (0,)), ((), ()))


def _alibi_slopes(n):
    return [2.0 ** (-8.0 * (i + 1) / n) for i in range(n)]


def _params(sem, vmem=V7X_VMEM_LIMIT_BYTES):
    return pltpu.CompilerParams(dimension_semantics=sem, vmem_limit_bytes=vmem)


def _rms_proj_kernel(*refs, norm_tiles, with_dt):
    if with_dt:
        x_ref, nw_ref, w_ref, gw_ref, nm_ref, g_ref, wd_ref, o_ref, dt_ref, h_ref = refs
    else:
        x_ref, nw_ref, w_ref, gw_ref, nm_ref, g_ref, o_ref, h_ref = refs
    j = pl.program_id(1)

    @pl.when(j == 0)
    def _():
        x = x_ref[...]
        ms = jnp.mean(x * x, axis=-1, keepdims=True)
        h_ref[...] = (x * lax.rsqrt(ms + EPS) * nw_ref[...]).astype(BF16)
        if with_dt:
            dt_ref[...] = jnp.dot(h_ref[...], wd_ref[...], preferred_element_type=F32)

    acc = jnp.dot(h_ref[...], w_ref[...], preferred_element_type=F32)
    needs_norm = functools.reduce(jnp.logical_or, [j == t for t in norm_tiles])

    @pl.when(needs_norm)
    def _():
        ss = jnp.dot((acc * acc).astype(BF16), g_ref[...], preferred_element_type=F32)
        inv = lax.rsqrt(ss * (1.0 / HEAD_DIM) + EPS)
        fac = jnp.where(nm_ref[...] > 0.0, inv, 1.0) * gw_ref[...]
        o_ref[...] = (acc * fac).astype(o_ref.dtype)

    @pl.when(jnp.logical_not(needs_norm))
    def _():
        o_ref[...] = acc.astype(o_ref.dtype)


def _rms_proj(x2d, norm_w, w, gw, nm, gmat, norm_tiles, wd=None, tm=1024):
    m, d = x2d.shape
    n = w.shape[1]
    tn = GROUP_TILE
    tm = min(tm, m)
    assert m % tm == 0 and n % tn == 0
    with_dt = wd is not None
    in_specs = [
        pl.BlockSpec((tm, d), lambda i, j: (i, 0)),
        pl.BlockSpec((1, d), lambda i, j: (0, 0)),
        pl.BlockSpec((d, tn), lambda i, j: (0, j)),
        pl.BlockSpec((1, tn), lambda i, j: (0, j)),
        pl.BlockSpec((1, tn), lambda i, j: (0, j)),
        pl.BlockSpec((tn, tn), lambda i, j: (0, 0)),
    ]
    args = [x2d, norm_w, w, gw, nm, gmat]
    out_shape = [jax.ShapeDtypeStruct((m, n), BF16)]
    out_specs = [pl.BlockSpec((tm, tn), lambda i, j: (i, j))]
    if with_dt:
        nd = wd.shape[1]
        in_specs.append(pl.BlockSpec((d, nd), lambda i, j: (0, 0)))
        args.append(wd)
        out_shape.append(jax.ShapeDtypeStruct((m, nd), F32))
        out_specs.append(pl.BlockSpec((tm, nd), lambda i, j: (i, 0)))
    res = pl.pallas_call(
        functools.partial(_rms_proj_kernel, norm_tiles=tuple(norm_tiles), with_dt=with_dt),
        grid=(m // tm, n // tn),
        in_specs=in_specs,
        out_specs=out_specs,
        out_shape=out_shape,
        scratch_shapes=[pltpu.VMEM((tm, d), BF16)],
        compiler_params=_params(("arbitrary", "arbitrary")),
        name="rms_proj_dt" if with_dt else "rms_proj",
    )(*args)
    return res


def _out_mlp_kernel(*refs, n_mix):
    x_ref = refs[0]
    y_refs = refs[1:1 + n_mix]
    wo_refs = refs[1 + n_mix:1 + 2 * n_mix]
    nw_ref, w1_ref, w2_ref, o_ref, hn_ref = refs[1 + 2 * n_mix:]
    f = pl.program_id(1)

    @pl.when(f == 0)
    def _():
        x1 = x_ref[...]
        for y_ref, wo_ref in zip(y_refs, wo_refs):
            x1 = x1 + jnp.dot(y_ref[...], wo_ref[...], preferred_element_type=F32)
        ms = jnp.mean(x1 * x1, axis=-1, keepdims=True)
        hn_ref[...] = (x1 * lax.rsqrt(ms + EPS) * nw_ref[...]).astype(BF16)
        o_ref[...] = x1

    hid = jnp.dot(hn_ref[...], w1_ref[...], preferred_element_type=F32)
    act = jnp.square(jnp.maximum(hid, 0.0)).astype(BF16)
    o_ref[...] += jnp.dot(act, w2_ref[...], preferred_element_type=F32)


def _out_mlp(x2d, ys, wos, norm_w, w1, w2, tm=512, tf=512):
    m, d = x2d.shape
    dff = w1.shape[1]
    tm = min(tm, m)
    assert m % tm == 0 and dff % tf == 0
    n_mix = len(ys)
    in_specs = [pl.BlockSpec((tm, d), lambda i, f: (i, 0))]
    in_specs += [pl.BlockSpec((tm, y.shape[1]), lambda i, f: (i, 0)) for y in ys]
    in_specs += [pl.BlockSpec(wo.shape, lambda i, f: (0, 0)) for wo in wos]
    in_specs += [
        pl.BlockSpec((1, d), lambda i, f: (0, 0)),
        pl.BlockSpec((d, tf), lambda i, f: (0, f)),
        pl.BlockSpec((tf, d), lambda i, f: (f, 0)),
    ]
    return pl.pallas_call(
        functools.partial(_out_mlp_kernel, n_mix=n_mix),
        grid=(m // tm, dff // tf),
        in_specs=in_specs,
        out_specs=pl.BlockSpec((tm, d), lambda i, f: (i, 0)),
        out_shape=jax.ShapeDtypeStruct((m, d), F32),
        scratch_shapes=[pltpu.VMEM((tm, d), BF16)],
        compiler_params=_params(("arbitrary", "arbitrary")),
        name="out_mlp",
    )(x2d, *ys, *wos, norm_w, w1, w2)


def _swa_kernel(sink_ref, q_ref, kp_ref, kc_ref, vp_ref, vc_ref, o_ref, bias_ref):
    n = pl.program_id(1)
    blk = SWA_BLOCK
    rows = lax.broadcasted_iota(jnp.int32, (blk, 2 * blk), 0)
    cols = lax.broadcasted_iota(jnp.int32, (blk, 2 * blk), 1)

    @pl.when(jnp.logical_and(pl.program_id(0) == 0, n == 0))
    def _():
        dist = jnp.abs(rows + blk - cols).astype(F32)
        dchunk = rows // CHUNK + 2 - cols // CHUNK
        valid = jnp.logical_and(dchunk >= 0, dchunk <= SWA_BLOCK // CHUNK)
        for h, slope in enumerate(_alibi_slopes(SWA_HEADS)):
            bias_ref[h] = jnp.where(valid, -slope * dist, -jnp.inf)

    first = jnp.where(jnp.logical_and(n == 0, cols < blk), -jnp.inf, 0.0)
    lane = lax.broadcasted_iota(jnp.int32, (1, LANES), 1)
    lo = lane < HEAD_DIM
    zero = jnp.zeros((), BF16)

    pairs_per_kv = SWA_HEADS // SWA_KV_HEADS // 2
    for kh in range(SWA_KV_HEADS):
        c0 = (kh // 2) * LANES
        k128 = jnp.concatenate([kp_ref[0, :, c0:c0 + LANES], kc_ref[0, :, c0:c0 + LANES]], axis=0)
        v128 = jnp.concatenate([vp_ref[0, :, c0:c0 + LANES], vc_ref[0, :, c0:c0 + LANES]], axis=0)
        k_sw = pltpu.roll(k128, HEAD_DIM, 1)
        v_sw = pltpu.roll(v128, HEAD_DIM, 1)
        k_lo, k_hi = (k128, k_sw) if kh % 2 == 0 else (k_sw, k128)
        v_lo, v_hi = (v128, v_sw) if kh % 2 == 0 else (v_sw, v128)
        v_lo = jnp.where(lo, v_lo, zero)
        v_hi = jnp.where(lo, zero, v_hi)
        for pr in range(pairs_per_kv):
            pair = kh * pairs_per_kv + pr
            q128 = q_ref[0, :, pair * LANES:(pair + 1) * LANES]
            out = None
            for half, (kk, vv) in enumerate(((k_lo, v_lo), (k_hi, v_hi))):
                h = 2 * pair + half
                qh = jnp.where(lo, q128, zero) if half == 0 else jnp.where(lo, zero, q128)
                s = lax.dot_general(qh, kk, NT_DIMS, preferred_element_type=F32)
                s = s + bias_ref[h] + first
                sink = sink_ref[h]
                m = jnp.maximum(jnp.max(s, axis=-1, keepdims=True), sink)
                ex = jnp.exp(s - m)
                den = jnp.sum(ex, axis=-1, keepdims=True) + jnp.exp(sink - m)
                prob = (ex / den).astype(BF16)
                o_h = jnp.dot(prob, vv, preferred_element_type=F32)
                out = o_h if out is None else out + o_h
            o_ref[0, :, pair * LANES:(pair + 1) * LANES] = out.astype(o_ref.dtype)


def _swa(proj3, sinks, q_col, k_col, v_col):
    b, t, _ = proj3.shape
    blk = SWA_BLOCK
    nb = t // blk
    dq = SWA_HEADS * HEAD_DIM
    dkv = SWA_KV_HEADS * HEAD_DIM
    qb, kb, vb = q_col // dq, k_col // dkv, v_col // dkv
    prev = lambda n: jnp.maximum(n - 1, 0)
    return pl.pallas_call(
        _swa_kernel,
        grid=(b, nb),
        in_specs=[
            pl.BlockSpec(memory_space=pltpu.SMEM),
            pl.BlockSpec((1, blk, dq), lambda i, n: (i, n, qb)),
            pl.BlockSpec((1, blk, dkv), lambda i, n: (i, prev(n), kb)),
            pl.BlockSpec((1, blk, dkv), lambda i, n: (i, n, kb)),
            pl.BlockSpec((1, blk, dkv), lambda i, n: (i, prev(n), vb)),
            pl.BlockSpec((1, blk, dkv), lambda i, n: (i, n, vb)),
        ],
        out_specs=pl.BlockSpec((1, blk, dq), lambda i, n: (i, n, 0)),
        out_shape=jax.ShapeDtypeStruct((b, t, dq), BF16),
        scratch_shapes=[pltpu.VMEM((SWA_HEADS, blk, 2 * blk), F32)],
        compiler_params=_params(("arbitrary", "arbitrary")),
        name="swa",
    )(sinks, proj3, proj3, proj3, proj3, proj3)


def _split_bf16(a):
    hi = a.astype(BF16)
    lo = (a - hi.astype(F32)).astype(BF16)
    return hi, lo


def _ssd_kernel(z_ref, xs_ref, bc_ref, dt_ref, cw_ref, cb_ref, dtb_ref, alog_ref, dskip_ref, nw_ref,
                o_ref, xbuf_ref, st_ref):
    tt = SSD_TILE
    inner = SSD_INNER
    gw = inner // SSD_GROUPS
    ns = SSD_STATE
    t = pl.program_id(1)

    @pl.when(t == 0)
    def _():
        st_ref[...] = jnp.zeros_like(st_ref)
        xbuf_ref[0:8, :] = jnp.zeros((8, xbuf_ref.shape[1]), F32)

    @pl.when(t > 0)
    def _():
        xbuf_ref[0:8, :] = xbuf_ref[tt:tt + 8, :]

    xbuf_ref[8:8 + tt, 0:inner] = xs_ref[0].astype(F32)
    xbuf_ref[8:8 + tt, inner:] = bc_ref[0].astype(F32)

    conv = cb_ref[...] + cw_ref[0:1, :] * xbuf_ref[5:5 + tt, :]
    for k in range(1, SSD_CONV):
        conv = conv + cw_ref[k:k + 1, :] * xbuf_ref[5 + k:5 + k + tt, :]
    xbc = conv * jax.nn.sigmoid(conv)
    xs = xbc[:, 0:inner]
    bm = xbc[:, inner:inner + SSD_GROUPS * ns].astype(BF16)
    cm = xbc[:, inner + SSD_GROUPS * ns:].astype(BF16)

    dtx = dt_ref[0] + dtb_ref[...]
    dt = jnp.maximum(dtx, 0.0) + jnp.log1p(jnp.exp(-jnp.abs(dtx)))
    a_dt = dt * (-jnp.exp(alog_ref[...]))

    ri = lax.broadcasted_iota(jnp.int32, (tt, tt), 0)
    ci = lax.broadcasted_iota(jnp.int32, (tt, tt), 1)
    tril = jnp.logical_and(ri // CHUNK == ci // CHUNK, ci <= ri).astype(BF16)
    rl = lax.broadcasted_iota(jnp.int32, (tt, 1), 0) % CHUNK
    cl = lax.broadcasted_iota(jnp.int32, (1, inner), 1) % HEAD_DIM

    hi, lo = _split_bf16(a_dt)
    a_cum = jnp.dot(tril, hi, preferred_element_type=F32) + jnp.dot(tril, lo, preferred_element_type=F32)
    hi, lo = _split_bf16(jnp.where(rl > cl, a_dt, 0.0))
    seg = jnp.dot(tril, hi, preferred_element_type=F32) + jnp.dot(tril, lo, preferred_element_type=F32)
    decay = jnp.exp(jnp.where(cl <= rl, seg, -jnp.inf))

    xdt = xs * dt
    quad = lax.broadcasted_iota(jnp.int32, (CHUNK, 4 * HEAD_DIM), 1) // HEAD_DIM
    zero = jnp.zeros((), BF16)

    for c in range(tt // CHUNK):
        r0 = c * CHUNK
        rs = slice(r0, r0 + CHUNK)
        acum_c = a_cum[rs]
        a_last = a_cum[r0 + CHUNK - 1:r0 + CHUNK]
        xdt_c = xdt[rs]
        xdt_b = xdt_c.astype(BF16)
        cbs = []
        for g in range(SSD_GROUPS):
            b_g = bm[rs, g * ns:(g + 1) * ns]
            c_g = cm[rs, g * ns:(g + 1) * ns]
            b_tiled = jnp.concatenate([b_g] * (gw // HEAD_DIM), axis=0)
            cbs.append(lax.dot_general(c_g, b_tiled, NT_DIMS, preferred_element_type=F32))
        gmat = (jnp.concatenate(cbs, axis=1) * decay[rs]).astype(BF16)
        yd = []
        for qd in range(inner // (4 * HEAD_DIM)):
            cs = slice(qd * 4 * HEAD_DIM, (qd + 1) * 4 * HEAD_DIM)
            xq = xdt_b[:, cs]
            xbd = jnp.concatenate([jnp.where(quad == jj, xq, zero) for jj in range(4)], axis=0)
            yd.append(jnp.dot(gmat[:, cs], xbd, preferred_element_type=F32))
        y = jnp.concatenate(yd, axis=1)
        din = jnp.exp(acum_c)
        x_end = (xdt_c * jnp.exp(a_last - acum_c)).astype(BF16)
        cdec = jnp.exp(a_last)
        yo = []
        for g in range(SSD_GROUPS):
            gs = slice(g * gw, (g + 1) * gw)
            b_g = bm[rs, g * ns:(g + 1) * ns]
            c_g = cm[rs, g * ns:(g + 1) * ns]
            st_g = st_ref[:, gs]
            yo.append(jnp.dot(c_g, st_g.astype(BF16), preferred_element_type=F32))
            upd = lax.dot_general(b_g, x_end[:, gs], TN_DIMS, preferred_element_type=F32)
            st_ref[:, gs] = st_g * cdec[:, gs] + upd
        y = y + jnp.concatenate(yo, axis=1) * din
        y = y + xs[rs] * dskip_ref[...]
        zz = z_ref[0, rs, :].astype(F32)
        y = y * (zz * jax.nn.sigmoid(zz))
        outs = []
        for g in range(SSD_GROUPS):
            y_g = y[:, g * gw:(g + 1) * gw]
            ms = jnp.mean(y_g * y_g, axis=-1, keepdims=True)
            outs.append(y_g * lax.rsqrt(ms + EPS))
        o_ref[0, rs, :] = (jnp.concatenate(outs, axis=1) * nw_ref[...]).astype(o_ref.dtype)


def _ssd(proj3, dtx3, conv_w, conv_b, dtb, alog, dskip, nw, z_col, xs_col, bc_col):
    b, t, _ = proj3.shape
    tt = SSD_TILE
    assert t % tt == 0
    inner = SSD_INNER
    bcw = 2 * SSD_GROUPS * SSD_STATE
    cdim = inner + bcw
    full = lambda shape: pl.BlockSpec(shape, lambda i, s: (0,) * len(shape))
    return pl.pallas_call(
        _ssd_kernel,
        grid=(b, t // tt),
        in_specs=[
            pl.BlockSpec((1, tt, inner), lambda i, s: (i, s, z_col // inner)),
            pl.BlockSpec((1, tt, inner), lambda i, s: (i, s, xs_col // inner)),
            pl.BlockSpec((1, tt, bcw), lambda i, s: (i, s, bc_col // bcw)),
            pl.BlockSpec((1, tt, inner), lambda i, s: (i, s, 0)),
            full((SSD_CONV, cdim)), full((1, cdim)), full((1, inner)), full((1, inner)),
            full((1, inner)), full((1, inner)),
        ],
        out_specs=pl.BlockSpec((1, tt, inner), lambda i, s: (i, s, 0)),
        out_shape=jax.ShapeDtypeStruct((b, t, inner), BF16),
        scratch_shapes=[pltpu.VMEM((tt + 8, cdim), F32), pltpu.VMEM((SSD_STATE, inner), F32)],
        compiler_params=_params(("arbitrary", "arbitrary")),
        name="ssd",
    )(proj3, proj3, proj3, dtx3, conv_w, conv_b, dtb, alog, dskip, nw)


def _diff_attn_kernel(slope_ref, lam_ref, q_ref, k_ref, v_ref, sn_ref, o_ref, boff_ref, bdiag_ref, *,
                      lam_init, nq):
    blk = DIFF_BLOCK
    slope = jnp.full((1, 1), slope_ref[pl.program_id(1)], F32)
    rows = lax.broadcasted_iota(jnp.int32, (blk, blk), 0)
    cols = lax.broadcasted_iota(jnp.int32, (blk, blk), 1)
    rel = (rows - cols).astype(F32)
    boff_ref[...] = -slope * rel
    bdiag_ref[...] = jnp.where(cols // CHUNK <= rows // CHUNK, -slope * jnp.abs(rel), -jnp.inf)

    lam_p = lam_ref[...]
    lam = (jnp.exp(jnp.sum(lam_p[0:1] * lam_p[1:2], axis=-1, keepdims=True))
           - jnp.exp(jnp.sum(lam_p[2:3] * lam_p[3:4], axis=-1, keepdims=True)) + lam_init)

    lane = lax.broadcasted_iota(jnp.int32, (1, LANES), 1)
    lo = lane < HEAD_DIM
    zero = jnp.zeros((), BF16)

    def block(qm, k, v, bias_ref, shift, carry):
        m_old, l_old, acc = carry
        s = lax.dot_general(qm, k, NT_DIMS, preferred_element_type=F32) + bias_ref[...]
        m_new = jnp.maximum(m_old, jnp.max(s, axis=-1, keepdims=True) + shift)
        p = jnp.exp(s - (m_new - shift))
        alpha = jnp.exp(m_old - m_new)
        l_new = alpha * l_old + jnp.sum(p, axis=-1, keepdims=True)
        acc = alpha * acc + jnp.dot(p.astype(BF16), v, preferred_element_type=F32)
        return m_new, l_new, acc

    def q_step(qi, _):
        q0 = pl.multiple_of(qi * blk, blk)
        q = q_ref[0, pl.ds(q0, blk), :]
        q1 = jnp.where(lo, q, zero)
        q2 = jnp.where(lo, zero, q)
        k_d = k_ref[0, pl.ds(q0, blk), :]
        v_d = v_ref[0, pl.ds(q0, blk), :]
        init = (jnp.full((blk, 1), -jnp.inf, F32), jnp.zeros((blk, 1), F32), jnp.zeros((blk, LANES), F32))
        c1 = block(q1, k_d, v_d, bdiag_ref, 0.0, init)
        c2 = block(q2, k_d, v_d, bdiag_ref, 0.0, init)

        def kv_step(j, carry):
            c1, c2 = carry
            k0 = pl.multiple_of(j * blk, blk)
            k = k_ref[0, pl.ds(k0, blk), :]
            v = v_ref[0, pl.ds(k0, blk), :]
            shift = -slope * jnp.full((1, 1), (qi - j) * blk, jnp.int32).astype(F32)
            return block(q1, k, v, boff_ref, shift, c1), block(q2, k, v, boff_ref, shift, c2)

        (m1, l1, a1), (m2, l2, a2) = lax.fori_loop(0, qi, kv_step, (c1, c2))
        o = a1 / l1 - lam * (a2 / l2)
        ms = jnp.mean(o * o, axis=-1, keepdims=True)
        o = o * lax.rsqrt(ms + EPS) * sn_ref[...] * (1.0 - lam_init)
        o_ref[0, pl.ds(q0, blk), :] = o.astype(o_ref.dtype)
        return 0

    lax.fori_loop(0, nq, q_step, 0)


def _diff_attn(proj3, lam_p, sub_norm, lam_init):
    b, t, _ = proj3.shape
    blk = DIFF_BLOCK
    assert t % blk == 0
    hd = 2 * HEAD_DIM
    nh = DIFF_HEADS
    return pl.pallas_call(
        functools.partial(_diff_attn_kernel, lam_init=lam_init, nq=t // blk),
        grid=(b, nh),
        in_specs=[
            pl.BlockSpec(memory_space=pltpu.SMEM),
            pl.BlockSpec((4, HEAD_DIM), lambda i, h: (0, 0)),
            pl.BlockSpec((1, t, hd), lambda i, h: (i, 0, h)),
            pl.BlockSpec((1, t, hd), lambda i, h: (i, 0, nh + h)),
            pl.BlockSpec((1, t, hd), lambda i, h: (i, 0, 2 * nh + h)),
            pl.BlockSpec((1, hd), lambda i, h: (0, 0)),
        ],
        out_specs=pl.BlockSpec((1, t, hd), lambda i, h: (i, 0, h)),
        out_shape=jax.ShapeDtypeStruct((b, t, nh * hd), BF16),
        scratch_shapes=[pltpu.VMEM((blk, blk), F32), pltpu.VMEM((blk, blk), F32)],
        compiler_params=_params(("arbitrary", "arbitrary")),
        name="diff_attn",
    )(jnp.asarray(_alibi_slopes(nh), F32), lam_p, proj3, proj3, proj3, sub_norm)


def _group_sum_matrix():
    idx = jnp.arange(GROUP_TILE) // HEAD_DIM
    return (idx[:, None] == idx[None, :]).astype(BF16)


def kernel(x, ev_norm_w, ev_w_in, ev_conv_w, ev_conv_b, ev_dt_bias, ev_a_log, ev_d_skip, ev_ssd_norm_w,
           ev_q_norm, ev_k_norm, ev_sinks, ev_w_out,
           od_norm_w, od_w_in, od_q_norm, od_k_norm, od_lam_q1, od_lam_k1, od_lam_q2, od_lam_k2,
           od_sub_norm, od_w_out, mlp_norm_w, mlp_w1, mlp_w2):
    bsz, t, d = x.shape
    m = bsz * t
    depth = mlp_w1.shape[0]
    gmat = _group_sum_matrix()
    x2 = x.reshape(m, d)
    for layer in range(depth):
        j = layer // 2
        if layer % 2 == 0:
            w_in = ev_w_in[j]
            inner = SSD_INNER
            bcw = 2 * SSD_GROUPS * SSD_STATE
            dq = SWA_HEADS * HEAD_DIM
            dkv = SWA_KV_HEADS * HEAD_DIM
            o_xbc = inner
            o_dt = o_xbc + inner + bcw
            o_q = o_dt + SSD_HEADS
            o_k = o_q + dq
            o_v = o_k + dkv
            w_re = jnp.concatenate([
                w_in[:, 0:inner], w_in[:, o_xbc:o_xbc + inner], w_in[:, o_q:o_q + dq],
                w_in[:, o_xbc + inner:o_dt], w_in[:, o_k:o_k + dkv], w_in[:, o_v:o_v + dkv]], axis=1).astype(BF16)
            z_col, xs_col, q_col, bc_col = 0, inner, 2 * inner, 2 * inner + dq
            k_col = bc_col + bcw
            v_col = k_col + dkv
            n_re = v_col + dkv
            w_dt = jnp.repeat(w_in[:, o_dt:o_dt + SSD_HEADS], HEAD_DIM, axis=1).astype(BF16)
            ones = lambda n: jnp.ones((n,), F32)
            gw = jnp.concatenate([ones(q_col), jnp.tile(ev_q_norm[j], SWA_HEADS) * (HEAD_DIM ** -0.5), ones(bcw),
                                  jnp.tile(ev_k_norm[j], SWA_KV_HEADS), ones(dkv)]).reshape(1, n_re)
            nm = jnp.concatenate([jnp.zeros((q_col,), F32), ones(dq), jnp.zeros((bcw,), F32), ones(dkv),
                                  jnp.zeros((dkv,), F32)]).reshape(1, n_re)
            norm_tiles = sorted({c // GROUP_TILE for c in list(range(q_col, q_col + dq, HEAD_DIM))
                                 + list(range(k_col, k_col + dkv, HEAD_DIM))})
            proj, dtx = _rms_proj(x2, ev_norm_w[j].reshape(1, d), w_re, gw, nm, gmat, norm_tiles, wd=w_dt)
            proj3 = proj.reshape(bsz, t, n_re)
            rep = lambda v: jnp.repeat(v, HEAD_DIM).reshape(1, inner)
            y_ssd = _ssd(proj3, dtx.reshape(bsz, t, inner), ev_conv_w[j], ev_conv_b[j].reshape(1, -1),
                         rep(ev_dt_bias[j]), rep(ev_a_log[j]), rep(ev_d_skip[j]),
                         ev_ssd_norm_w[j].reshape(1, inner), z_col, xs_col, bc_col)
            y_swa = _swa(proj3, ev_sinks[j], q_col, k_col, v_col)
            w_out = ev_w_out[j].astype(BF16)
            ys = [y_ssd.reshape(m, inner), y_swa.reshape(m, dq)]
            wos = [w_out[0:inner], w_out[inner:]]
        else:
            w_in = od_w_in[j].astype(BF16)
            dm = 2 * DIFF_HEADS * HEAD_DIM
            ones = jnp.ones((dm,), F32)
            gw = jnp.concatenate([jnp.tile(od_q_norm[j], 2 * DIFF_HEADS) * (HEAD_DIM ** -0.5),
                                  jnp.tile(od_k_norm[j], 2 * DIFF_HEADS), ones]).reshape(1, 3 * dm)
            nm = jnp.concatenate([ones, ones, jnp.zeros((dm,), F32)]).reshape(1, 3 * dm)
            norm_tiles = list(range(2 * dm // GROUP_TILE))
            (proj,) = _rms_proj(x2, od_norm_w[j].reshape(1, d), w_in, gw, nm, gmat, norm_tiles)
            lam_init = 0.8 - 0.6 * math.exp(-0.3 * layer)
            lam_p = jnp.stack([od_lam_q1[j], od_lam_k1[j], od_lam_q2[j], od_lam_k2[j]])
            y = _diff_attn(proj.reshape(bsz, t, 3 * dm), lam_p, od_sub_norm[j].reshape(1, -1), lam_init)
            ys = [y.reshape(m, dm)]
            wos = [od_w_out[j].astype(BF16)]
        x2 = _out_mlp(x2, ys, wos, mlp_norm_w[layer].reshape(1, d), mlp_w1[layer].astype(BF16),
                      mlp_w2[layer].astype(BF16))
    return x2.reshape(bsz, t, d)
```

```python
import functools
import math

import jax
import jax.numpy as jnp
from jax import lax
from jax.experimental import pallas as pl
from jax.experimental.pallas import tpu as pltpu

F32 = jnp.float32
BF16 = jnp.bfloat16

LANES = 128
V7X_VMEM_LIMIT_BYTES = 56 * 1024 * 1024

EPS = 1e-6
CHUNK = 64
HEAD_DIM = 64
GROUP_TILE = 512
SSD_HEADS = 16
SSD_STATE = 128
SSD_GROUPS = 2
SSD_INNER = SSD_HEADS * HEAD_DIM
SSD_CONV = 4
SWA_HEADS = 16
SWA_KV_HEADS = 4
SWA_BLOCK = 128
DIFF_HEADS = 8
DIFF_BLOCK = 512
DIFF_STRIP = 32
SSD_TILE = 256

NT_DIMS = (((1,), (1,)), ((), ()))
TN_DIMS = (((0,), (0,)), ((), ()))


def _alibi_slopes(n):
    return [2.0 ** (-8.0 * (i + 1) / n) for i in range(n)]


def _params(sem, vmem=V7X_VMEM_LIMIT_BYTES):
    return pltpu.CompilerParams(dimension_semantics=sem, vmem_limit_bytes=vmem)


def _rms_proj_kernel(*refs, norm_tiles, with_dt):
    if with_dt:
        x_ref, nw_ref, w_ref, gw_ref, nm_ref, g_ref, wd_ref, o_ref, dt_ref, h_ref = refs
    else:
        x_ref, nw_ref, w_ref, gw_ref, nm_ref, g_ref, o_ref, h_ref = refs
    tn = GROUP_TILE
    x = x_ref[...]
    ms = jnp.mean(x * x, axis=-1, keepdims=True)
    h_ref[...] = (x * lax.rsqrt(ms + EPS) * nw_ref[...]).astype(BF16)
    if with_dt:
        dt_ref[...] = jnp.dot(h_ref[...], wd_ref[...], preferred_element_type=F32)
    for j in range(w_ref.shape[1] // tn):
        cs = slice(j * tn, (j + 1) * tn)
        acc = jnp.dot(h_ref[...], w_ref[:, cs], preferred_element_type=F32)
        if j in norm_tiles:
            ss = jnp.dot((acc * acc).astype(BF16), g_ref[...], preferred_element_type=F32)
            inv = lax.rsqrt(ss * (1.0 / HEAD_DIM) + EPS)
            acc = acc * (jnp.where(nm_ref[:, cs] > 0.0, inv, 1.0) * gw_ref[:, cs])
        o_ref[:, cs] = acc.astype(o_ref.dtype)


def _resident(shape):
    return pl.BlockSpec(shape, lambda i: (0,) * len(shape), pipeline_mode=pl.Buffered(1))


def _rms_proj(x2d, norm_w, w, gw, nm, gmat, norm_tiles, wd=None, tm=512):
    m, d = x2d.shape
    n = w.shape[1]
    tm = min(tm, m)
    assert m % tm == 0 and n % GROUP_TILE == 0
    with_dt = wd is not None
    in_specs = [pl.BlockSpec((tm, d), lambda i: (i, 0)), _resident((1, d)), _resident((d, n)),
                _resident((1, n)), _resident((1, n)), _resident(gmat.shape)]
    args = [x2d, norm_w, w, gw, nm, gmat]
    out_shape = [jax.ShapeDtypeStruct((m, n), BF16)]
    out_specs = [pl.BlockSpec((tm, n), lambda i: (i, 0))]
    if with_dt:
        nd = wd.shape[1]
        in_specs.append(_resident((d, nd)))
        args.append(wd)
        out_shape.append(jax.ShapeDtypeStruct((m, nd), F32))
        out_specs.append(pl.BlockSpec((tm, nd), lambda i: (i, 0)))
    res = pl.pallas_call(
        functools.partial(_rms_proj_kernel, norm_tiles=tuple(norm_tiles), with_dt=with_dt),
        grid=(m // tm,),
        in_specs=in_specs,
        out_specs=out_specs,
        out_shape=out_shape,
        scratch_shapes=[pltpu.VMEM((tm, d), BF16)],
        compiler_params=_params(("arbitrary",)),
        name="rms_proj_dt" if with_dt else "rms_proj",
    )(*args)
    return res


def _out_mlp_kernel(*refs, n_mix, tf):
    x_ref = refs[0]
    y_refs = refs[1:1 + n_mix]
    wo_refs = refs[1 + n_mix:1 + 2 * n_mix]
    nw_ref, w1_ref, w2_ref, o_ref, hn_ref = refs[1 + 2 * n_mix:]
    x1 = x_ref[...]
    for y_ref, wo_ref in zip(y_refs, wo_refs):
        x1 = x1 + jnp.dot(y_ref[...], wo_ref[...], preferred_element_type=F32)
    ms = jnp.mean(x1 * x1, axis=-1, keepdims=True)
    hn_ref[...] = (x1 * lax.rsqrt(ms + EPS) * nw_ref[...]).astype(BF16)
    o_ref[...] = x1
    for f in range(w1_ref.shape[1] // tf):
        fs = slice(f * tf, (f + 1) * tf)
        hid = jnp.dot(hn_ref[...], w1_ref[:, fs], preferred_element_type=F32)
        act = jnp.square(jnp.maximum(hid, 0.0)).astype(BF16)
        o_ref[...] += jnp.dot(act, w2_ref[fs, :], preferred_element_type=F32)


def _out_mlp(x2d, ys, wos, norm_w, w1, w2, tm=512, tf=512):
    m, d = x2d.shape
    dff = w1.shape[1]
    tm = min(tm, m)
    assert m % tm == 0 and dff % tf == 0
    n_mix = len(ys)
    in_specs = [pl.BlockSpec((tm, d), lambda i: (i, 0))]
    in_specs += [pl.BlockSpec((tm, y.shape[1]), lambda i: (i, 0)) for y in ys]
    in_specs += [_resident(wo.shape) for wo in wos]
    in_specs += [_resident((1, d)), _resident(w1.shape), _resident(w2.shape)]
    return pl.pallas_call(
        functools.partial(_out_mlp_kernel, n_mix=n_mix, tf=tf),
        grid=(m // tm,),
        in_specs=in_specs,
        out_specs=pl.BlockSpec((tm, d), lambda i: (i, 0)),
        out_shape=jax.ShapeDtypeStruct((m, d), F32),
        scratch_shapes=[pltpu.VMEM((tm, d), BF16)],
        compiler_params=_params(("arbitrary",)),
        name="out_mlp",
    )(x2d, *ys, *wos, norm_w, w1, w2)


def _swa_kernel(sink_ref, q_ref, kp_ref, kc_ref, vp_ref, vc_ref, o_ref, bias_ref):
    n = pl.program_id(1)
    blk = SWA_BLOCK
    rows = lax.broadcasted_iota(jnp.int32, (blk, 2 * blk), 0)
    cols = lax.broadcasted_iota(jnp.int32, (blk, 2 * blk), 1)

    @pl.when(jnp.logical_and(pl.program_id(0) == 0, n == 0))
    def _():
        dist = jnp.abs(rows + blk - cols).astype(F32)
        dchunk = rows // CHUNK + 2 - cols // CHUNK
        valid = jnp.logical_and(dchunk >= 0, dchunk <= SWA_BLOCK // CHUNK)
        for h, slope in enumerate(_alibi_slopes(SWA_HEADS)):
            bias_ref[h * blk:(h + 1) * blk, :] = jnp.where(valid, -slope * dist, -jnp.inf)

    rep = SWA_HEADS // SWA_KV_HEADS
    first = jnp.where(jnp.logical_and(n == 0, cols < blk), -jnp.inf, 0.0)
    first = jnp.concatenate([first] * rep, axis=0)
    lane = lax.broadcasted_iota(jnp.int32, (1, LANES), 1)
    lo = lane < HEAD_DIM
    zero = jnp.zeros((), BF16)
    head_row = lax.broadcasted_iota(jnp.int32, (rep * blk, 1), 0) // blk

    for kh in range(SWA_KV_HEADS):
        c0 = (kh // 2) * LANES
        k128 = jnp.concatenate([kp_ref[0, :, c0:c0 + LANES], kc_ref[0, :, c0:c0 + LANES]], axis=0)
        v128 = jnp.concatenate([vp_ref[0, :, c0:c0 + LANES], vc_ref[0, :, c0:c0 + LANES]], axis=0)
        k_sw = pltpu.roll(k128, HEAD_DIM, 1)
        v_sw = pltpu.roll(v128, HEAD_DIM, 1)
        k_both = jnp.where(lo, k128, k_sw) if kh % 2 == 0 else jnp.where(lo, k_sw, k128)
        v_both = jnp.where(lo, v128, v_sw) if kh % 2 == 0 else jnp.where(lo, v_sw, v128)
        parts = []
        for pr in range(rep // 2):
            pair = kh * (rep // 2) + pr
            q128 = q_ref[0, :, pair * LANES:(pair + 1) * LANES]
            parts += [jnp.where(lo, q128, zero), jnp.where(lo, zero, q128)]
        q4 = jnp.concatenate(parts, axis=0)
        s = lax.dot_general(q4, k_both, NT_DIMS, preferred_element_type=F32)
        s = s + bias_ref[kh * rep * blk:(kh + 1) * rep * blk, :] + first
        sink = jnp.zeros((rep * blk, 1), F32)
        for r in range(rep):
            sink = jnp.where(head_row == r, sink_ref[kh * rep + r], sink)
        m = jnp.maximum(jnp.max(s, axis=-1, keepdims=True), sink)
        ex = jnp.exp(s - m)
        den = jnp.sum(ex, axis=-1, keepdims=True) + jnp.exp(sink - m)
        o4 = jnp.dot(ex.astype(BF16), v_both, preferred_element_type=F32) * (1.0 / den)
        for pr in range(rep // 2):
            pair = kh * (rep // 2) + pr
            out = jnp.where(lo, o4[2 * pr * blk:(2 * pr + 1) * blk], o4[(2 * pr + 1) * blk:(2 * pr + 2) * blk])
            o_ref[0, :, pair * LANES:(pair + 1) * LANES] = out.astype(o_ref.dtype)


def _swa(proj3, sinks, q_col, k_col, v_col):
    b, t, _ = proj3.shape
    blk = SWA_BLOCK
    nb = t // blk
    dq = SWA_HEADS * HEAD_DIM
    dkv = SWA_KV_HEADS * HEAD_DIM
    qb, kb, vb = q_col // dq, k_col // dkv, v_col // dkv
    prev = lambda n: jnp.maximum(n - 1, 0)
    return pl.pallas_call(
        _swa_kernel,
        grid=(b, nb),
        in_specs=[
            pl.BlockSpec(memory_space=pltpu.SMEM),
            pl.BlockSpec((1, blk, dq), lambda i, n: (i, n, qb)),
            pl.BlockSpec((1, blk, dkv), lambda i, n: (i, prev(n), kb)),
            pl.BlockSpec((1, blk, dkv), lambda i, n: (i, n, kb)),
            pl.BlockSpec((1, blk, dkv), lambda i, n: (i, prev(n), vb)),
            pl.BlockSpec((1, blk, dkv), lambda i, n: (i, n, vb)),
        ],
        out_specs=pl.BlockSpec((1, blk, dq), lambda i, n: (i, n, 0)),
        out_shape=jax.ShapeDtypeStruct((b, t, dq), BF16),
        scratch_shapes=[pltpu.VMEM((SWA_HEADS * blk, 2 * blk), F32)],
        compiler_params=_params(("arbitrary", "arbitrary")),
        name="swa",
    )(sinks, proj3, proj3, proj3, proj3, proj3)


def _split_bf16(a):
    hi = a.astype(BF16)
    lo = (a - hi.astype(F32)).astype(BF16)
    return hi, lo


def _ssd_kernel(z_ref, xs_ref, bc_ref, dt_ref, cw_ref, cb_ref, dtb_ref, alog_ref, dskip_ref, nw_ref,
                o_ref, xbuf_ref, st_ref):
    tt = SSD_TILE
    inner = SSD_INNER
    gw = inner // SSD_GROUPS
    ns = SSD_STATE
    t = pl.program_id(1)

    @pl.when(t == 0)
    def _():
        st_ref[...] = jnp.zeros_like(st_ref)
        xbuf_ref[0:8, :] = jnp.zeros((8, xbuf_ref.shape[1]), F32)

    @pl.when(t > 0)
    def _():
        xbuf_ref[0:8, :] = xbuf_ref[tt:tt + 8, :]

    xbuf_ref[8:8 + tt, 0:inner] = xs_ref[0].astype(F32)
    xbuf_ref[8:8 + tt, inner:] = bc_ref[0].astype(F32)

    conv = cb_ref[...] + cw_ref[0:1, :] * xbuf_ref[5:5 + tt, :]
    for k in range(1, SSD_CONV):
        conv = conv + cw_ref[k:k + 1, :] * xbuf_ref[5 + k:5 + k + tt, :]
    xbc = conv * jax.nn.sigmoid(conv)
    xs = xbc[:, 0:inner]
    bm = xbc[:, inner:inner + SSD_GROUPS * ns].astype(BF16)
    cm = xbc[:, inner + SSD_GROUPS * ns:].astype(BF16)

    dtx = dt_ref[0] + dtb_ref[...]
    dt = jnp.maximum(dtx, 0.0) + jnp.log1p(jnp.exp(-jnp.abs(dtx)))
    a_dt = dt * (-jnp.exp(alog_ref[...]))

    ri = lax.broadcasted_iota(jnp.int32, (tt, tt), 0)
    ci = lax.broadcasted_iota(jnp.int32, (tt, tt), 1)
    tril = jnp.logical_and(ri // CHUNK == ci // CHUNK, ci <= ri).astype(BF16)
    rl = lax.broadcasted_iota(jnp.int32, (tt, 1), 0) % CHUNK
    cl = lax.broadcasted_iota(jnp.int32, (1, inner), 1) % HEAD_DIM

    hi, lo = _split_bf16(a_dt)
    a_cum = jnp.dot(tril, hi, preferred_element_type=F32) + jnp.dot(tril, lo, preferred_element_type=F32)
    hi, lo = _split_bf16(jnp.where(rl > cl, a_dt, 0.0))
    seg = jnp.dot(tril, hi, preferred_element_type=F32) + jnp.dot(tril, lo, preferred_element_type=F32)
    decay = jnp.exp(jnp.where(cl <= rl, seg, -jnp.inf))

    xdt = xs * dt
    quad = lax.broadcasted_iota(jnp.int32, (CHUNK, 4 * HEAD_DIM), 1) // HEAD_DIM
    zero = jnp.zeros((), BF16)

    for c in range(tt // CHUNK):
        r0 = c * CHUNK
        rs = slice(r0, r0 + CHUNK)
        acum_c = a_cum[rs]
        a_last = a_cum[r0 + CHUNK - 1:r0 + CHUNK]
        xdt_c = xdt[rs]
        xdt_b = xdt_c.astype(BF16)
        cbs = []
        for g in range(SSD_GROUPS):
            b_g = bm[rs, g * ns:(g + 1) * ns]
            c_g = cm[rs, g * ns:(g + 1) * ns]
            b_tiled = jnp.concatenate([b_g] * (gw // HEAD_DIM), axis=0)
            cbs.append(lax.dot_general(c_g, b_tiled, NT_DIMS, preferred_element_type=F32))
        gmat = (jnp.concatenate(cbs, axis=1) * decay[rs]).astype(BF16)
        yd = []
        for qd in range(inner // (4 * HEAD_DIM)):
            cs = slice(qd * 4 * HEAD_DIM, (qd + 1) * 4 * HEAD_DIM)
            xq = xdt_b[:, cs]
            xbd = jnp.concatenate([jnp.where(quad == jj, xq, zero) for jj in range(4)], axis=0)
            yd.append(jnp.dot(gmat[:, cs], xbd, preferred_element_type=F32))
        y = jnp.concatenate(yd, axis=1)
        din = jnp.exp(acum_c)
        x_end = (xdt_c * jnp.exp(a_last - acum_c)).astype(BF16)
        cdec = jnp.exp(a_last)
        yo = []
        for g in range(SSD_GROUPS):
            gs = slice(g * gw, (g + 1) * gw)
            b_g = bm[rs, g * ns:(g + 1) * ns]
            c_g = cm[rs, g * ns:(g + 1) * ns]
            st_g = st_ref[:, gs]
            yo.append(jnp.dot(c_g, st_g.astype(BF16), preferred_element_type=F32))
            upd = lax.dot_general(b_g, x_end[:, gs], TN_DIMS, preferred_element_type=F32)
            st_ref[:, gs] = st_g * cdec[:, gs] + upd
        y = y + jnp.concatenate(yo, axis=1) * din
        y = y + xs[rs] * dskip_ref[...]
        zz = z_ref[0, rs, :].astype(F32)
        y = y * (zz * jax.nn.sigmoid(zz))
        outs = []
        for g in range(SSD_GROUPS):
            y_g = y[:, g * gw:(g + 1) * gw]
            ms = jnp.mean(y_g * y_g, axis=-1, keepdims=True)
            outs.append(y_g * lax.rsqrt(ms + EPS))
        o_ref[0, rs, :] = (jnp.concatenate(outs, axis=1) * nw_ref[...]).astype(o_ref.dtype)


def _ssd(proj3, dtx3, conv_w, conv_b, dtb, alog, dskip, nw, z_col, xs_col, bc_col):
    b, t, _ = proj3.shape
    tt = SSD_TILE
    assert t % tt == 0
    inner = SSD_INNER
    bcw = 2 * SSD_GROUPS * SSD_STATE
    cdim = inner + bcw
    full = lambda shape: pl.BlockSpec(shape, lambda i, s: (0,) * len(shape))
    return pl.pallas_call(
        _ssd_kernel,
        grid=(b, t // tt),
        in_specs=[
            pl.BlockSpec((1, tt, inner), lambda i, s: (i, s, z_col // inner)),
            pl.BlockSpec((1, tt, inner), lambda i, s: (i, s, xs_col // inner)),
            pl.BlockSpec((1, tt, bcw), lambda i, s: (i, s, bc_col // bcw)),
            pl.BlockSpec((1, tt, inner), lambda i, s: (i, s, 0)),
            full((SSD_CONV, cdim)), full((1, cdim)), full((1, inner)), full((1, inner)),
            full((1, inner)), full((1, inner)),
        ],
        out_specs=pl.BlockSpec((1, tt, inner), lambda i, s: (i, s, 0)),
        out_shape=jax.ShapeDtypeStruct((b, t, inner), BF16),
        scratch_shapes=[pltpu.VMEM((tt + 8, cdim), F32), pltpu.VMEM((SSD_STATE, inner), F32)],
        compiler_params=_params(("arbitrary", "arbitrary")),
        name="ssd",
    )(proj3, proj3, proj3, dtx3, conv_w, conv_b, dtb, alog, dskip, nw)


def _diff_attn_kernel(slope_ref, lam_ref, q_ref, k_ref, v_ref, sn_ref, o_ref, boff_ref, bdiag_ref,
                      qm_ref, s_ref, p_ref, m_ref, l_ref, al_ref, acc_ref, *, lam_init, nq):
    blk = DIFF_BLOCK
    strip = DIFF_STRIP
    slope = jnp.full((1, 1), slope_ref[pl.program_id(1)], F32)
    rows = lax.broadcasted_iota(jnp.int32, (blk, blk), 0)
    cols = lax.broadcasted_iota(jnp.int32, (blk, blk), 1)
    rel = (rows - cols).astype(F32)
    boff_ref[...] = -slope * rel
    bdiag_ref[...] = jnp.where(cols // CHUNK <= rows // CHUNK, -slope * jnp.abs(rel), -jnp.inf)

    lam_p = lam_ref[...]
    lam = (jnp.exp(jnp.sum(lam_p[0:1] * lam_p[1:2], axis=-1, keepdims=True))
           - jnp.exp(jnp.sum(lam_p[2:3] * lam_p[3:4], axis=-1, keepdims=True)) + lam_init)

    lane = lax.broadcasted_iota(jnp.int32, (1, LANES), 1)
    lo = lane < HEAD_DIM
    zero = jnp.zeros((), BF16)

    def scores(br, k, bias_ref):
        s_ref[br] = lax.dot_general(qm_ref[br], k, NT_DIMS, preferred_element_type=F32) + bias_ref[...]

    def softmax(br, shift, first):
        nl = blk // LANES
        wide = lambda a: jnp.concatenate([a] * nl, axis=1)
        for r in range(0, blk, strip):
            rs = slice(r, r + strip)
            mx = jnp.broadcast_to(jnp.max(s_ref[br, rs, :], axis=-1, keepdims=True), (strip, LANES))
            if first:
                m_ref[br, rs, :] = mx
            else:
                m_old = m_ref[br, rs, :]
                m_new = jnp.maximum(m_old, mx + shift)
                al_ref[br, rs, :] = jnp.exp(m_old - m_new)
                m_ref[br, rs, :] = m_new
        for r in range(0, blk, strip):
            rs = slice(r, r + strip)
            m_new = m_ref[br, rs, :]
            p = jnp.exp(s_ref[br, rs, :] - wide(m_new if first else m_new - shift))
            lp = p[:, 0:LANES]
            for c in range(1, nl):
                lp = lp + p[:, c * LANES:(c + 1) * LANES]
            l_ref[br, rs, :] = lp if first else al_ref[br, rs, :] * l_ref[br, rs, :] + lp
            p_ref[br, rs, :] = p.astype(BF16)

    def accumulate(br, v, first):
        pv = jnp.dot(p_ref[br], v, preferred_element_type=F32)
        acc_ref[br] = pv if first else al_ref[br] * acc_ref[br] + pv

    def q_step(qi, _):
        q0 = pl.multiple_of(qi * blk, blk)
        q = q_ref[0, pl.ds(q0, blk), :]
        qm_ref[0] = jnp.where(lo, q, zero)
        qm_ref[1] = jnp.where(lo, zero, q)
        k_d = k_ref[0, pl.ds(q0, blk), :]
        v_d = v_ref[0, pl.ds(q0, blk), :]
        scores(0, k_d, bdiag_ref)
        scores(1, k_d, bdiag_ref)
        softmax(0, None, True)
        accumulate(0, v_d, True)
        softmax(1, None, True)
        accumulate(1, v_d, True)

        def kv_step(j, _):
            k0 = pl.multiple_of(j * blk, blk)
            k = k_ref[0, pl.ds(k0, blk), :]
            v = v_ref[0, pl.ds(k0, blk), :]
            shift = -slope * jnp.full((1, 1), (qi - j) * blk, jnp.int32).astype(F32)
            scores(0, k, boff_ref)
            scores(1, k, boff_ref)
            softmax(0, shift, False)
            accumulate(0, v, False)
            softmax(1, shift, False)
            accumulate(1, v, False)
            return 0

        lax.fori_loop(0, qi, kv_step, 0)
        l1 = jnp.sum(l_ref[0], axis=-1, keepdims=True)
        l2 = jnp.sum(l_ref[1], axis=-1, keepdims=True)
        o = acc_ref[0] / l1 - lam * (acc_ref[1] / l2)
        ms = jnp.mean(o * o, axis=-1, keepdims=True)
        o = o * lax.rsqrt(ms + EPS) * sn_ref[...] * (1.0 - lam_init)
        o_ref[0, pl.ds(q0, blk), :] = o.astype(o_ref.dtype)
        return 0

    lax.fori_loop(0, nq, q_step, 0)


def _diff_attn(proj3, lam_p, sub_norm, lam_init):
    b, t, _ = proj3.shape
    blk = DIFF_BLOCK
    assert t % blk == 0
    hd = 2 * HEAD_DIM
    nh = DIFF_HEADS
    return pl.pallas_call(
        functools.partial(_diff_attn_kernel, lam_init=lam_init, nq=t // blk),
        grid=(b, nh),
        in_specs=[
            pl.BlockSpec(memory_space=pltpu.SMEM),
            pl.BlockSpec((4, HEAD_DIM), lambda i, h: (0, 0)),
            pl.BlockSpec((1, t, hd), lambda i, h: (i, 0, h)),
            pl.BlockSpec((1, t, hd), lambda i, h: (i, 0, nh + h)),
            pl.BlockSpec((1, t, hd), lambda i, h: (i, 0, 2 * nh + h)),
            pl.BlockSpec((1, hd), lambda i, h: (0, 0)),
        ],
        out_specs=pl.BlockSpec((1, t, hd), lambda i, h: (i, 0, h)),
        out_shape=jax.ShapeDtypeStruct((b, t, nh * hd), BF16),
        scratch_shapes=[
            pltpu.VMEM((blk, blk), F32), pltpu.VMEM((blk, blk), F32),
            pltpu.VMEM((2, blk, hd), BF16), pltpu.VMEM((2, blk, blk), F32), pltpu.VMEM((2, blk, blk), BF16),
            pltpu.VMEM((2, blk, LANES), F32), pltpu.VMEM((2, blk, LANES), F32), pltpu.VMEM((2, blk, LANES), F32),
            pltpu.VMEM((2, blk, hd), F32),
        ],
        compiler_params=_params(("arbitrary", "arbitrary")),
        name="diff_attn",
    )(jnp.asarray(_alibi_slopes(nh), F32), lam_p, proj3, proj3, proj3, sub_norm)


def _group_sum_matrix():
    idx = jnp.arange(GROUP_TILE) // HEAD_DIM
    return (idx[:, None] == idx[None, :]).astype(BF16)


def kernel(x, ev_norm_w, ev_w_in, ev_conv_w, ev_conv_b, ev_dt_bias, ev_a_log, ev_d_skip, ev_ssd_norm_w,
           ev_q_norm, ev_k_norm, ev_sinks, ev_w_out,
           od_norm_w, od_w_in, od_q_norm, od_k_norm, od_lam_q1, od_lam_k1, od_lam_q2, od_lam_k2,
           od_sub_norm, od_w_out, mlp_norm_w, mlp_w1, mlp_w2):
    bsz, t, d = x.shape
    m = bsz * t
    depth = mlp_w1.shape[0]
    gmat = _group_sum_matrix()
    x2 = x.reshape(m, d)
    for layer in range(depth):
        j = layer // 2
        if layer % 2 == 0:
            w_in = ev_w_in[j]
            inner = SSD_INNER
            bcw = 2 * SSD_GROUPS * SSD_STATE
            dq = SWA_HEADS * HEAD_DIM
            dkv = SWA_KV_HEADS * HEAD_DIM
            o_xbc = inner
            o_dt = o_xbc + inner + bcw
            o_q = o_dt + SSD_HEADS
            o_k = o_q + dq
            o_v = o_k + dkv
            w_re = jnp.concatenate([
                w_in[:, 0:inner], w_in[:, o_xbc:o_xbc + inner], w_in[:, o_q:o_q + dq],
                w_in[:, o_xbc + inner:o_dt], w_in[:, o_k:o_k + dkv], w_in[:, o_v:o_v + dkv]], axis=1).astype(BF16)
            z_col, xs_col, q_col, bc_col = 0, inner, 2 * inner, 2 * inner + dq
            k_col = bc_col + bcw
            v_col = k_col + dkv
            n_re = v_col + dkv
            w_dt = jnp.repeat(w_in[:, o_dt:o_dt + SSD_HEADS], HEAD_DIM, axis=1).astype(BF16)
            ones = lambda n: jnp.ones((n,), F32)
            gw = jnp.concatenate([ones(q_col), jnp.tile(ev_q_norm[j], SWA_HEADS) * (HEAD_DIM ** -0.5), ones(bcw),
                                  jnp.tile(ev_k_norm[j], SWA_KV_HEADS), ones(dkv)]).reshape(1, n_re)
            nm = jnp.concatenate([jnp.zeros((q_col,), F32), ones(dq), jnp.zeros((bcw,), F32), ones(dkv),
                                  jnp.zeros((dkv,), F32)]).reshape(1, n_re)
            norm_tiles = sorted({c // GROUP_TILE for c in list(range(q_col, q_col + dq, HEAD_DIM))
                                 + list(range(k_col, k_col + dkv, HEAD_DIM))})
            proj, dtx = _rms_proj(x2, ev_norm_w[j].reshape(1, d), w_re, gw, nm, gmat, norm_tiles, wd=w_dt)
            proj3 = proj.reshape(bsz, t, n_re)
            rep = lambda v: jnp.repeat(v, HEAD_DIM).reshape(1, inner)
            y_ssd = _ssd(proj3, dtx.reshape(bsz, t, inner), ev_conv_w[j], ev_conv_b[j].reshape(1, -1),
                         rep(ev_dt_bias[j]), rep(ev_a_log[j]), rep(ev_d_skip[j]),
                         ev_ssd_norm_w[j].reshape(1, inner), z_col, xs_col, bc_col)
            y_swa = _swa(proj3, ev_sinks[j], q_col, k_col, v_col)
            w_out = ev_w_out[j].astype(BF16)
            ys = [y_ssd.reshape(m, inner), y_swa.reshape(m, dq)]
            wos = [w_out[0:inner], w_out[inner:]]
        else:
            w_in = od_w_in[j].astype(BF16)
            dm = 2 * DIFF_HEADS * HEAD_DIM
            ones = jnp.ones((dm,), F32)
            gw = jnp.concatenate([jnp.tile(od_q_norm[j], 2 * DIFF_HEADS) * (HEAD_DIM ** -0.5),
                                  jnp.tile(od_k_norm[j], 2 * DIFF_HEADS), ones]).reshape(1, 3 * dm)
            nm = jnp.concatenate([ones, ones, jnp.zeros((dm,), F32)]).reshape(1, 3 * dm)
            norm_tiles = list(range(2 * dm // GROUP_TILE))
            (proj,) = _rms_proj(x2, od_norm_w[j].reshape(1, d), w_in, gw, nm, gmat, norm_tiles)
            lam_init = 0.8 - 0.6 * math.exp(-0.3 * layer)
            lam_p = jnp.stack([od_lam_q1[j], od_lam_k1[j], od_lam_q2[j], od_lam_k2[j]])
            y = _diff_attn(proj.reshape(bsz, t, 3 * dm), lam_p, od_sub_norm[j].reshape(1, -1), lam_init)
            ys = [y.reshape(m, dm)]
            wos = [od_w_out[j].astype(BF16)]
        x2 = _out_mlp(x2, ys, wos, mlp_norm_w[layer].reshape(1, d), mlp_w1[layer].astype(BF16),
                      mlp_w2[layer].astype(BF16))
    return x2.reshape(bsz, t, d)
```

```python
import functools
import math
import struct

import jax
import jax.numpy as jnp
from jax import lax
from jax.experimental import pallas as pl
from jax.experimental.pallas import tpu as pltpu

F32 = jnp.float32
BF16 = jnp.bfloat16

LANES = 128
V7X_VMEM_LIMIT_BYTES = 56 * 1024 * 1024

EPS = 1e-6
CHUNK = 64
HEAD_DIM = 64
GROUP_TILE = 512
SSD_HEADS = 16
SSD_STATE = 128
SSD_GROUPS = 2
SSD_INNER = SSD_HEADS * HEAD_DIM
SSD_CONV = 4
SWA_HEADS = 16
SWA_KV_HEADS = 4
SWA_BLOCK = 128
DIFF_HEADS = 8
DIFF_BLOCK = 512
DIFF_STRIP = 32
SSD_TILE = 256

NT_DIMS = (((1,), (1,)), ((), ()))
TN_DIMS = (((0,), (0,)), ((), ()))


def _alibi_slopes(n):
    return [2.0 ** (-8.0 * (i + 1) / n) for i in range(n)]


def _params(sem, vmem=V7X_VMEM_LIMIT_BYTES):
    return pltpu.CompilerParams(dimension_semantics=sem, vmem_limit_bytes=vmem)


def _rms_proj_kernel(*refs, norm_tiles, with_dt):
    if with_dt:
        x_ref, nw_ref, w_ref, gw_ref, nm_ref, g_ref, wd_ref, o_ref, dt_ref, h_ref = refs
    else:
        x_ref, nw_ref, w_ref, gw_ref, nm_ref, g_ref, o_ref, h_ref = refs
    tn = GROUP_TILE
    x = x_ref[...]
    ms = jnp.mean(x * x, axis=-1, keepdims=True)
    h_ref[...] = (x * lax.rsqrt(ms + EPS) * nw_ref[...]).astype(BF16)
    if with_dt:
        dt_ref[...] = jnp.dot(h_ref[...], wd_ref[...], preferred_element_type=F32)
    for j in range(w_ref.shape[1] // tn):
        cs = slice(j * tn, (j + 1) * tn)
        acc = jnp.dot(h_ref[...], w_ref[:, cs], preferred_element_type=F32)
        if j in norm_tiles:
            ss = jnp.dot((acc * acc).astype(BF16), g_ref[...], preferred_element_type=F32)
            inv = lax.rsqrt(ss * (1.0 / HEAD_DIM) + EPS)
            acc = acc * (jnp.where(nm_ref[:, cs] > 0.0, inv, 1.0) * gw_ref[:, cs])
        o_ref[:, cs] = acc.astype(o_ref.dtype)


def _resident(shape):
    return pl.BlockSpec(shape, lambda i: (0,) * len(shape), pipeline_mode=pl.Buffered(1))


def _rms_proj(x2d, norm_w, w, gw, nm, gmat, norm_tiles, wd=None, tm=512):
    m, d = x2d.shape
    n = w.shape[1]
    tm = min(tm, m)
    assert m % tm == 0 and n % GROUP_TILE == 0
    with_dt = wd is not None
    in_specs = [pl.BlockSpec((tm, d), lambda i: (i, 0)), _resident((1, d)), _resident((d, n)),
                _resident((1, n)), _resident((1, n)), _resident(gmat.shape)]
    args = [x2d, norm_w, w, gw, nm, gmat]
    out_shape = [jax.ShapeDtypeStruct((m, n), BF16)]
    out_specs = [pl.BlockSpec((tm, n), lambda i: (i, 0))]
    if with_dt:
        nd = wd.shape[1]
        in_specs.append(_resident((d, nd)))
        args.append(wd)
        out_shape.append(jax.ShapeDtypeStruct((m, nd), F32))
        out_specs.append(pl.BlockSpec((tm, nd), lambda i: (i, 0)))
    res = pl.pallas_call(
        functools.partial(_rms_proj_kernel, norm_tiles=tuple(norm_tiles), with_dt=with_dt),
        grid=(m // tm,),
        in_specs=in_specs,
        out_specs=out_specs,
        out_shape=out_shape,
        scratch_shapes=[pltpu.VMEM((tm, d), BF16)],
        compiler_params=_params(("arbitrary",)),
        name="rms_proj_dt" if with_dt else "rms_proj",
    )(*args)
    return res


def _out_mlp_kernel(*refs, n_mix, tf):
    x_ref = refs[0]
    y_refs = refs[1:1 + n_mix]
    wo_refs = refs[1 + n_mix:1 + 2 * n_mix]
    nw_ref, w1_ref, w2_ref, o_ref, hn_ref = refs[1 + 2 * n_mix:]
    x1 = x_ref[...]
    for y_ref, wo_ref in zip(y_refs, wo_refs):
        x1 = x1 + jnp.dot(y_ref[...], wo_ref[...], preferred_element_type=F32)
    ms = jnp.mean(x1 * x1, axis=-1, keepdims=True)
    hn_ref[...] = (x1 * lax.rsqrt(ms + EPS) * nw_ref[...]).astype(BF16)
    o_ref[...] = x1
    for f in range(w1_ref.shape[1] // tf):
        fs = slice(f * tf, (f + 1) * tf)
        hid = jnp.dot(hn_ref[...], w1_ref[:, fs], preferred_element_type=F32)
        act = jnp.square(jnp.maximum(hid, 0.0)).astype(BF16)
        o_ref[...] += jnp.dot(act, w2_ref[fs, :], preferred_element_type=F32)


def _out_mlp(x2d, ys, wos, norm_w, w1, w2, tm=512, tf=512):
    m, d = x2d.shape
    dff = w1.shape[1]
    tm = min(tm, m)
    assert m % tm == 0 and dff % tf == 0
    n_mix = len(ys)
    in_specs = [pl.BlockSpec((tm, d), lambda i: (i, 0))]
    in_specs += [pl.BlockSpec((tm, y.shape[1]), lambda i: (i, 0)) for y in ys]
    in_specs += [_resident(wo.shape) for wo in wos]
    in_specs += [_resident((1, d)), _resident(w1.shape), _resident(w2.shape)]
    return pl.pallas_call(
        functools.partial(_out_mlp_kernel, n_mix=n_mix, tf=tf),
        grid=(m // tm,),
        in_specs=in_specs,
        out_specs=pl.BlockSpec((tm, d), lambda i: (i, 0)),
        out_shape=jax.ShapeDtypeStruct((m, d), F32),
        scratch_shapes=[pltpu.VMEM((tm, d), BF16)],
        compiler_params=_params(("arbitrary",)),
        name="out_mlp",
    )(x2d, *ys, *wos, norm_w, w1, w2)


def _swa_kernel(sink_ref, q_ref, kp_ref, kc_ref, vp_ref, vc_ref, o_ref, bias_ref, s_ref, p_ref, rden_ref):
    n = pl.program_id(1)
    blk = SWA_BLOCK
    rows = lax.broadcasted_iota(jnp.int32, (blk, 2 * blk), 0)
    cols = lax.broadcasted_iota(jnp.int32, (blk, 2 * blk), 1)

    @pl.when(jnp.logical_and(pl.program_id(0) == 0, n == 0))
    def _():
        dist = jnp.abs(rows + blk - cols).astype(F32)
        dchunk = rows // CHUNK + 2 - cols // CHUNK
        valid = jnp.logical_and(dchunk >= 0, dchunk <= SWA_BLOCK // CHUNK)
        for h, slope in enumerate(_alibi_slopes(SWA_HEADS)):
            bias_ref[h * blk:(h + 1) * blk, :] = jnp.where(valid, (-LOG2E * slope) * dist, -jnp.inf)

    rep = SWA_HEADS // SWA_KV_HEADS
    first = jnp.where(jnp.logical_and(n == 0, cols < blk), -jnp.inf, 0.0)
    first = jnp.concatenate([first] * rep, axis=0)
    lane = lax.broadcasted_iota(jnp.int32, (1, LANES), 1)
    lo = lane < HEAD_DIM
    zero = jnp.zeros((), BF16)
    head_row = lax.broadcasted_iota(jnp.int32, (rep * blk, 1), 0) // blk

    v_both = []
    for kh in range(SWA_KV_HEADS):
        c0 = (kh // 2) * LANES
        k128 = jnp.concatenate([kp_ref[0, :, c0:c0 + LANES], kc_ref[0, :, c0:c0 + LANES]], axis=0)
        v128 = jnp.concatenate([vp_ref[0, :, c0:c0 + LANES], vc_ref[0, :, c0:c0 + LANES]], axis=0)
        k_sw = pltpu.roll(k128, HEAD_DIM, 1)
        v_sw = pltpu.roll(v128, HEAD_DIM, 1)
        k_both = jnp.where(lo, k128, k_sw) if kh % 2 == 0 else jnp.where(lo, k_sw, k128)
        v_both.append(jnp.where(lo, v128, v_sw) if kh % 2 == 0 else jnp.where(lo, v_sw, v128))
        parts = []
        for pr in range(rep // 2):
            pair = kh * (rep // 2) + pr
            q128 = q_ref[0, :, pair * LANES:(pair + 1) * LANES]
            parts += [jnp.where(lo, q128, zero), jnp.where(lo, zero, q128)]
        q4 = jnp.concatenate(parts, axis=0)
        s = lax.dot_general(q4, k_both, NT_DIMS, preferred_element_type=F32)
        s_ref[kh] = s + bias_ref[kh * rep * blk:(kh + 1) * rep * blk, :] + first
    for kh in range(SWA_KV_HEADS):
        sink = jnp.zeros((rep * blk, 1), F32)
        for r in range(rep):
            sink = jnp.where(head_row == r, sink_ref[kh * rep + r], sink)
        sink = sink * LOG2E
        s = s_ref[kh]
        m = jnp.maximum(jnp.max(s, axis=-1, keepdims=True), sink)
        ex = jnp.exp2(s - m)
        den = jnp.sum(ex, axis=-1, keepdims=True) + jnp.exp2(sink - m)
        p_ref[kh] = ex.astype(BF16)
        rden_ref[kh] = jnp.broadcast_to(1.0 / den, (rep * blk, LANES))
    for kh in range(SWA_KV_HEADS):
        o4 = jnp.dot(p_ref[kh], v_both[kh], preferred_element_type=F32) * rden_ref[kh]
        for pr in range(rep // 2):
            pair = kh * (rep // 2) + pr
            out = jnp.where(lo, o4[2 * pr * blk:(2 * pr + 1) * blk], o4[(2 * pr + 1) * blk:(2 * pr + 2) * blk])
            o_ref[0, :, pair * LANES:(pair + 1) * LANES] = out.astype(o_ref.dtype)


def _swa(proj3, sinks, q_col, k_col, v_col):
    b, t, _ = proj3.shape
    blk = SWA_BLOCK
    nb = t // blk
    dq = SWA_HEADS * HEAD_DIM
    dkv = SWA_KV_HEADS * HEAD_DIM
    qb, kb, vb = q_col // dq, k_col // dkv, v_col // dkv
    prev = lambda n: jnp.maximum(n - 1, 0)
    return pl.pallas_call(
        _swa_kernel,
        grid=(b, nb),
        in_specs=[
            pl.BlockSpec(memory_space=pltpu.SMEM),
            pl.BlockSpec((1, blk, dq), lambda i, n: (i, n, qb)),
            pl.BlockSpec((1, blk, dkv), lambda i, n: (i, prev(n), kb)),
            pl.BlockSpec((1, blk, dkv), lambda i, n: (i, n, kb)),
            pl.BlockSpec((1, blk, dkv), lambda i, n: (i, prev(n), vb)),
            pl.BlockSpec((1, blk, dkv), lambda i, n: (i, n, vb)),
        ],
        out_specs=pl.BlockSpec((1, blk, dq), lambda i, n: (i, n, 0)),
        out_shape=jax.ShapeDtypeStruct((b, t, dq), BF16),
        scratch_shapes=[
            pltpu.VMEM((SWA_HEADS * blk, 2 * blk), F32),
            pltpu.VMEM((SWA_KV_HEADS, SWA_HEADS // SWA_KV_HEADS * blk, 2 * blk), F32),
            pltpu.VMEM((SWA_KV_HEADS, SWA_HEADS // SWA_KV_HEADS * blk, 2 * blk), BF16),
            pltpu.VMEM((SWA_KV_HEADS, SWA_HEADS // SWA_KV_HEADS * blk, LANES), F32),
        ],
        compiler_params=_params(("arbitrary", "arbitrary")),
        name="swa",
    )(sinks, proj3, proj3, proj3, proj3, proj3)


def _split_bf16(a):
    hi = a.astype(BF16)
    lo = (a - hi.astype(F32)).astype(BF16)
    return hi, lo


def _ssd_kernel(z_ref, xs_ref, bc_ref, xsp_ref, bcp_ref, dt_ref, cw_ref, cb_ref, dtb_ref, alog_ref, dskip_ref,
                nw_ref, o_ref, st_ref):
    tt = SSD_TILE
    inner = SSD_INNER
    gw = inner // SSD_GROUPS
    ns = SSD_STATE
    t = pl.program_id(1)

    half = tt // 2

    @pl.when(t == 0)
    def _():
        st_ref[...] = jnp.zeros_like(st_ref)

    si = lax.broadcasted_iota(jnp.int32, ((SSD_CONV - 1) * half, tt), 0)
    sc = lax.broadcasted_iota(jnp.int32, ((SSD_CONV - 1) * half, tt), 1)
    sel = (sc == (si % half) + half - (SSD_CONV - 1) + si // half).astype(BF16)

    def conv_silu(cur_ref, prev_ref, c0, c1):
        outs = []
        for hh in range(2):
            cur = cur_ref[0, hh * half:(hh + 1) * half, :]
            if hh == 0:
                before = jnp.where(t > 0, prev_ref[0], jnp.zeros((), BF16))
            else:
                before = cur_ref[0, 0:half, :]
            sh = jnp.dot(sel, jnp.concatenate([before, cur], axis=0), preferred_element_type=F32)
            conv = cb_ref[:, c0:c1] + cw_ref[SSD_CONV - 1:SSD_CONV, c0:c1] * cur.astype(F32)
            for k in range(SSD_CONV - 1):
                conv = conv + cw_ref[k:k + 1, c0:c1] * sh[k * half:(k + 1) * half]
            outs.append(conv * jax.nn.sigmoid(conv))
        return jnp.concatenate(outs, axis=0)

    xs = conv_silu(xs_ref, xsp_ref, 0, inner)
    bcm = conv_silu(bc_ref, bcp_ref, inner, inner + 2 * SSD_GROUPS * ns).astype(BF16)
    bm = bcm[:, 0:SSD_GROUPS * ns]
    cm = bcm[:, SSD_GROUPS * ns:]

    dtx = dt_ref[0] + dtb_ref[...]
    dt = jnp.maximum(dtx, 0.0) + jnp.log(1.0 + jnp.exp(-jnp.abs(dtx)))
    a_dt = dt * (-jnp.exp(alog_ref[...]))

    ri = lax.broadcasted_iota(jnp.int32, (tt, tt), 0)
    ci = lax.broadcasted_iota(jnp.int32, (tt, tt), 1)
    tril = jnp.logical_and(ri // CHUNK == ci // CHUNK, ci <= ri).astype(BF16)
    rl = lax.broadcasted_iota(jnp.int32, (tt, 1), 0) % CHUNK
    cl = lax.broadcasted_iota(jnp.int32, (1, inner), 1) % HEAD_DIM

    hi, lo = _split_bf16(a_dt)
    a_cum = jnp.dot(tril, hi, preferred_element_type=F32) + jnp.dot(tril, lo, preferred_element_type=F32)
    hi, lo = _split_bf16(jnp.where(rl > cl, a_dt, 0.0))
    seg = jnp.dot(tril, hi, preferred_element_type=F32) + jnp.dot(tril, lo, preferred_element_type=F32)
    decay = jnp.exp(jnp.where(cl <= rl, seg, -jnp.inf))

    xdt = xs * dt
    quad = lax.broadcasted_iota(jnp.int32, (CHUNK, 4 * HEAD_DIM), 1) // HEAD_DIM
    zero = jnp.zeros((), BF16)

    for c in range(tt // CHUNK):
        r0 = c * CHUNK
        rs = slice(r0, r0 + CHUNK)
        acum_c = a_cum[rs]
        a_last = a_cum[r0 + CHUNK - 1:r0 + CHUNK]
        xdt_c = xdt[rs]
        xdt_b = xdt_c.astype(BF16)
        cbs = []
        for g in range(SSD_GROUPS):
            b_g = bm[rs, g * ns:(g + 1) * ns]
            c_g = cm[rs, g * ns:(g + 1) * ns]
            b_tiled = jnp.concatenate([b_g] * (gw // HEAD_DIM), axis=0)
            cbs.append(lax.dot_general(c_g, b_tiled, NT_DIMS, preferred_element_type=F32))
        gmat = (jnp.concatenate(cbs, axis=1) * decay[rs]).astype(BF16)
        yd = []
        for qd in range(inner // (4 * HEAD_DIM)):
            cs = slice(qd * 4 * HEAD_DIM, (qd + 1) * 4 * HEAD_DIM)
            xq = xdt_b[:, cs]
            xbd = jnp.concatenate([jnp.where(quad == jj, xq, zero) for jj in range(4)], axis=0)
            yd.append(jnp.dot(gmat[:, cs], xbd, preferred_element_type=F32))
        y = jnp.concatenate(yd, axis=1)
        din = jnp.exp(acum_c)
        x_end = (xdt_c * jnp.exp(a_last - acum_c)).astype(BF16)
        cdec = jnp.exp(a_last)
        yo = []
        for g in range(SSD_GROUPS):
            gs = slice(g * gw, (g + 1) * gw)
            b_g = bm[rs, g * ns:(g + 1) * ns]
            c_g = cm[rs, g * ns:(g + 1) * ns]
            st_g = st_ref[:, gs]
            yo.append(jnp.dot(c_g, st_g.astype(BF16), preferred_element_type=F32))
            upd = lax.dot_general(b_g, x_end[:, gs], TN_DIMS, preferred_element_type=F32)
            st_ref[:, gs] = st_g * cdec[:, gs] + upd
        y = y + jnp.concatenate(yo, axis=1) * din
        y = y + xs[rs] * dskip_ref[...]
        zz = z_ref[0, rs, :].astype(F32)
        y = y * (zz * jax.nn.sigmoid(zz))
        outs = []
        for g in range(SSD_GROUPS):
            y_g = y[:, g * gw:(g + 1) * gw]
            ms = jnp.mean(y_g * y_g, axis=-1, keepdims=True)
            outs.append(y_g * lax.rsqrt(ms + EPS))
        o_ref[0, rs, :] = (jnp.concatenate(outs, axis=1) * nw_ref[...]).astype(o_ref.dtype)


def _ssd(proj3, dtx3, conv_w, conv_b, dtb, alog, dskip, nw, z_col, xs_col, bc_col):
    b, t, _ = proj3.shape
    tt = SSD_TILE
    assert t % tt == 0
    inner = SSD_INNER
    bcw = 2 * SSD_GROUPS * SSD_STATE
    cdim = inner + bcw
    full = lambda shape: pl.BlockSpec(shape, lambda i, s: (0,) * len(shape))
    return pl.pallas_call(
        _ssd_kernel,
        grid=(b, t // tt),
        in_specs=[
            pl.BlockSpec((1, tt, inner), lambda i, s: (i, s, z_col // inner)),
            pl.BlockSpec((1, tt, inner), lambda i, s: (i, s, xs_col // inner)),
            pl.BlockSpec((1, tt, bcw), lambda i, s: (i, s, bc_col // bcw)),
            pl.BlockSpec((1, tt // 2, inner), lambda i, s: (i, jnp.maximum(2 * s - 1, 0), xs_col // inner)),
            pl.BlockSpec((1, tt // 2, bcw), lambda i, s: (i, jnp.maximum(2 * s - 1, 0), bc_col // bcw)),
            pl.BlockSpec((1, tt, inner), lambda i, s: (i, s, 0)),
            full((SSD_CONV, cdim)), full((1, cdim)), full((1, inner)), full((1, inner)),
            full((1, inner)), full((1, inner)),
        ],
        out_specs=pl.BlockSpec((1, tt, inner), lambda i, s: (i, s, 0)),
        out_shape=jax.ShapeDtypeStruct((b, t, inner), BF16),
        scratch_shapes=[pltpu.VMEM((SSD_STATE, inner), F32)],
        compiler_params=_params(("arbitrary", "arbitrary")),
        name="ssd",
    )(proj3, proj3, proj3, proj3, proj3, dtx3, conv_w, conv_b, dtb, alog, dskip, nw)


def _round_to_bf16(x):
    bits = struct.unpack("<I", struct.pack("<f", x))[0]
    bits = (bits + 0x7FFF + ((bits >> 16) & 1)) & 0xFFFF0000
    return struct.unpack("<f", struct.pack("<I", bits))[0]


def _bf16_terms(x, n):
    out = []
    for _ in range(n):
        out.append(_round_to_bf16(x))
        x -= out[-1]
    return out


LOG2E = math.log2(math.e)
LOG2E_TERMS = _bf16_terms(LOG2E, 3)
DIFF_BIAS_SPLIT = 16


def _diff_attn_kernel(slope_ref, lam_ref, q_ref, k_ref, v_ref, sn_ref, o_ref, tdiag_ref, kaug_ref, vt_ref,
                      qm_ref, s_ref, p_ref, m_ref, l_ref, acc_ref, *, lam_init, nq):
    blk = DIFF_BLOCK
    strip = DIFF_STRIP
    slope = jnp.full((1, 1), slope_ref[pl.program_id(1)], F32)
    kk = lax.broadcasted_iota(jnp.int32, (blk, blk), 0)
    qq = lax.broadcasted_iota(jnp.int32, (blk, blk), 1)
    tdiag_ref[...] = jnp.where(kk // CHUNK > qq // CHUNK, -jnp.inf,
                               jnp.where(kk > qq, (2.0 * LOG2E) * slope * (qq - kk).astype(F32), 0.0))

    lam_p = lam_ref[...]
    lam = (jnp.exp(jnp.sum(lam_p[0:1] * lam_p[1:2], axis=-1, keepdims=True))
           - jnp.exp(jnp.sum(lam_p[2:3] * lam_p[3:4], axis=-1, keepdims=True)) + lam_init)

    lane = lax.broadcasted_iota(jnp.int32, (blk, LANES), 1)
    pos = lax.broadcasted_iota(jnp.int32, (blk, LANES), 0)
    lo = lane < HEAD_DIM
    nt = len(LOG2E_TERMS)
    b_hi = slope * (DIFF_BIAS_SPLIT * (pos // DIFF_BIAS_SPLIT)).astype(F32)
    b_lo = slope * (pos % DIFF_BIAS_SPLIT).astype(F32)
    kbias, qconst = [], []
    for base in (HEAD_DIM, 0):
        rel = lane - base
        kbias.append(jnp.where((rel >= 0) & (rel < nt), b_hi,
                               jnp.where((rel >= nt) & (rel < 2 * nt), b_lo, 0.0)).astype(BF16))
        qc = jnp.zeros((blk, LANES), F32)
        for i, term in enumerate(LOG2E_TERMS):
            qc = jnp.where((rel == i) | (rel == nt + i), term, qc)
        qconst.append(qc.astype(BF16))
    for j in range(nq):
        k = k_ref[0, j * blk:(j + 1) * blk, :]
        kaug_ref[0, j] = jnp.where(lo, k, kbias[0])
        kaug_ref[1, j] = jnp.where(lo, kbias[1], k)
        vt_ref[j] = v_ref[0, j * blk:(j + 1) * blk, :].astype(F32).T.astype(BF16)

    def scores(slot, br, j, diag):
        s = lax.dot_general(kaug_ref[br, j], qm_ref[br], NT_DIMS, preferred_element_type=F32)
        if diag:
            s = s + tdiag_ref[...]
        s_ref[slot, br] = s
        return jnp.max(s, axis=0, keepdims=True)

    def absorb(slot, br, j, mx, shift):
        m_old = m_ref[br, 0:1, :]
        m_new = jnp.maximum(m_old, mx + shift)
        alpha = jnp.exp2(m_old - m_new)
        off = m_new - shift
        m_ref[br] = jnp.broadcast_to(m_new, m_ref.shape[1:])
        lp = [None, None]
        for i, r in enumerate(range(0, blk, strip)):
            p = jnp.exp2(s_ref[slot, br, r:r + strip, :] - off)
            part = p[0:8]
            for g in range(8, strip, 8):
                part = part + p[g:g + 8]
            lp[i % 2] = part if lp[i % 2] is None else lp[i % 2] + part
            p_ref[br, r:r + strip, :] = p.astype(BF16)
        l_ref[br] = alpha * l_ref[br] + (lp[0] + lp[1])
        pv = jnp.dot(vt_ref[j], p_ref[br], preferred_element_type=F32)
        acc_ref[br] = alpha * acc_ref[br] + pv

    def q_step(qi, _):
        q0 = pl.multiple_of(qi * blk, blk)
        q = q_ref[0, pl.ds(q0, blk), :]
        qm_ref[0] = jnp.where(lo, q, qconst[0])
        qm_ref[1] = jnp.where(lo, qconst[1], q)
        m_ref[...] = jnp.full(m_ref.shape, -jnp.inf, F32)
        l_ref[...] = jnp.zeros(l_ref.shape, F32)
        acc_ref[...] = jnp.zeros(acc_ref.shape, F32)
        pending = (jnp.zeros((1, 1), F32), scores(0, 0, qi, True), scores(0, 1, qi, True), qi)

        def finish(slot, pend):
            shift_p, mx0, mx1, j_p = pend
            absorb(slot, 0, j_p, mx0, shift_p)
            absorb(slot, 1, j_p, mx1, shift_p)

        def step(slot, pend, j):
            nxt = (scores(1 - slot, 0, j, False), scores(1 - slot, 1, j, False))
            finish(slot, pend)
            shift = (-LOG2E * blk) * slope * jnp.full((1, 1), qi - j, jnp.int32).astype(F32)
            return shift, nxt[0], nxt[1], j

        def kv_pair(t, pend):
            return step(1, step(0, pend, 2 * t), 2 * t + 1)

        pending = lax.fori_loop(0, qi // 2, kv_pair, pending)
        odd = qi % 2 == 1

        @pl.when(odd)
        def _():
            finish(1, step(0, pending, qi - 1))

        @pl.when(jnp.logical_not(odd))
        def _():
            finish(0, pending)

        l1 = jnp.sum(l_ref[0], axis=0, keepdims=True)
        l2 = jnp.sum(l_ref[1], axis=0, keepdims=True)
        o = (acc_ref[0] / l1 - lam * (acc_ref[1] / l2)).T
        ms = jnp.mean(o * o, axis=-1, keepdims=True)
        o = o * lax.rsqrt(ms + EPS) * sn_ref[...] * (1.0 - lam_init)
        o_ref[0, pl.ds(q0, blk), :] = o.astype(o_ref.dtype)
        return 0

    lax.fori_loop(0, nq, q_step, 0)


def _diff_attn(proj3, lam_p, sub_norm, lam_init):
    b, t, _ = proj3.shape
    blk = DIFF_BLOCK
    assert t % blk == 0
    hd = 2 * HEAD_DIM
    nh = DIFF_HEADS
    assert all(math.log2(s).is_integer() for s in _alibi_slopes(nh))
    return pl.pallas_call(
        functools.partial(_diff_attn_kernel, lam_init=lam_init, nq=t // blk),
        grid=(b, nh),
        in_specs=[
            pl.BlockSpec(memory_space=pltpu.SMEM),
            pl.BlockSpec((4, HEAD_DIM), lambda i, h: (0, 0)),
            pl.BlockSpec((1, t, hd), lambda i, h: (i, 0, h)),
            pl.BlockSpec((1, t, hd), lambda i, h: (i, 0, nh + h)),
            pl.BlockSpec((1, t, hd), lambda i, h: (i, 0, 2 * nh + h)),
            pl.BlockSpec((1, hd), lambda i, h: (0, 0)),
        ],
        out_specs=pl.BlockSpec((1, t, hd), lambda i, h: (i, 0, h)),
        out_shape=jax.ShapeDtypeStruct((b, t, nh * hd), BF16),
        scratch_shapes=[
            pltpu.VMEM((blk, blk), F32), pltpu.VMEM((2, t // blk, blk, hd), BF16),
            pltpu.VMEM((t // blk, hd, blk), BF16),
            pltpu.VMEM((2, blk, hd), BF16), pltpu.VMEM((2, 2, blk, blk), F32), pltpu.VMEM((2, blk, blk), BF16),
            pltpu.VMEM((2, 8, blk), F32), pltpu.VMEM((2, 8, blk), F32),
            pltpu.VMEM((2, hd, blk), F32),
        ],
        compiler_params=_params(("arbitrary", "arbitrary")),
        name="diff_attn",
    )(jnp.asarray(_alibi_slopes(nh), F32), lam_p, proj3, proj3, proj3, sub_norm)


def _group_sum_matrix():
    idx = jnp.arange(GROUP_TILE) // HEAD_DIM
    return (idx[:, None] == idx[None, :]).astype(BF16)


def kernel(x, ev_norm_w, ev_w_in, ev_conv_w, ev_conv_b, ev_dt_bias, ev_a_log, ev_d_skip, ev_ssd_norm_w,
           ev_q_norm, ev_k_norm, ev_sinks, ev_w_out,
           od_norm_w, od_w_in, od_q_norm, od_k_norm, od_lam_q1, od_lam_k1, od_lam_q2, od_lam_k2,
           od_sub_norm, od_w_out, mlp_norm_w, mlp_w1, mlp_w2):
    bsz, t, d = x.shape
    m = bsz * t
    depth = mlp_w1.shape[0]
    gmat = _group_sum_matrix()
    x2 = x.reshape(m, d)
    for layer in range(depth):
        j = layer // 2
        if layer % 2 == 0:
            w_in = ev_w_in[j]
            inner = SSD_INNER
            bcw = 2 * SSD_GROUPS * SSD_STATE
            dq = SWA_HEADS * HEAD_DIM
            dkv = SWA_KV_HEADS * HEAD_DIM
            o_xbc = inner
            o_dt = o_xbc + inner + bcw
            o_q = o_dt + SSD_HEADS
            o_k = o_q + dq
            o_v = o_k + dkv
            w_re = jnp.concatenate([
                w_in[:, 0:inner], w_in[:, o_xbc:o_xbc + inner], w_in[:, o_q:o_q + dq],
                w_in[:, o_xbc + inner:o_dt], w_in[:, o_k:o_k + dkv], w_in[:, o_v:o_v + dkv]], axis=1).astype(BF16)
            z_col, xs_col, q_col, bc_col = 0, inner, 2 * inner, 2 * inner + dq
            k_col = bc_col + bcw
            v_col = k_col + dkv
            n_re = v_col + dkv
            w_dt = jnp.repeat(w_in[:, o_dt:o_dt + SSD_HEADS], HEAD_DIM, axis=1).astype(BF16)
            ones = lambda n: jnp.ones((n,), F32)
            gw = jnp.concatenate([ones(q_col), jnp.tile(ev_q_norm[j], SWA_HEADS) * (LOG2E * HEAD_DIM ** -0.5), ones(bcw),
                                  jnp.tile(ev_k_norm[j], SWA_KV_HEADS), ones(dkv)]).reshape(1, n_re)
            nm = jnp.concatenate([jnp.zeros((q_col,), F32), ones(dq), jnp.zeros((bcw,), F32), ones(dkv),
                                  jnp.zeros((dkv,), F32)]).reshape(1, n_re)
            norm_tiles = sorted({c // GROUP_TILE for c in list(range(q_col, q_col + dq, HEAD_DIM))
                                 + list(range(k_col, k_col + dkv, HEAD_DIM))})
            proj, dtx = _rms_proj(x2, ev_norm_w[j].reshape(1, d), w_re, gw, nm, gmat, norm_tiles, wd=w_dt)
            proj3 = proj.reshape(bsz, t, n_re)
            rep = lambda v: jnp.repeat(v, HEAD_DIM).reshape(1, inner)
            y_ssd = _ssd(proj3, dtx.reshape(bsz, t, inner), ev_conv_w[j], ev_conv_b[j].reshape(1, -1),
                         rep(ev_dt_bias[j]), rep(ev_a_log[j]), rep(ev_d_skip[j]),
                         ev_ssd_norm_w[j].reshape(1, inner), z_col, xs_col, bc_col)
            y_swa = _swa(proj3, ev_sinks[j], q_col, k_col, v_col)
            w_out = ev_w_out[j].astype(BF16)
            ys = [y_ssd.reshape(m, inner), y_swa.reshape(m, dq)]
            wos = [w_out[0:inner], w_out[inner:]]
        else:
            w_in = od_w_in[j].astype(BF16)
            dm = 2 * DIFF_HEADS * HEAD_DIM
            ones = jnp.ones((dm,), F32)
            gw = jnp.concatenate([jnp.tile(od_q_norm[j], 2 * DIFF_HEADS) * (LOG2E * HEAD_DIM ** -0.5),
                                  jnp.tile(od_k_norm[j], 2 * DIFF_HEADS), ones]).reshape(1, 3 * dm)
            nm = jnp.concatenate([ones, ones, jnp.zeros((dm,), F32)]).reshape(1, 3 * dm)
            norm_tiles = list(range(2 * dm // GROUP_TILE))
            (proj,) = _rms_proj(x2, od_norm_w[j].reshape(1, d), w_in, gw, nm, gmat, norm_tiles)
            lam_init = 0.8 - 0.6 * math.exp(-0.3 * layer)
            lam_p = jnp.stack([od_lam_q1[j], od_lam_k1[j], od_lam_q2[j], od_lam_k2[j]])
            y = _diff_attn(proj.reshape(bsz, t, 3 * dm), lam_p, od_sub_norm[j].reshape(1, -1), lam_init)
            ys = [y.reshape(m, dm)]
            wos = [od_w_out[j].astype(BF16)]
        x2 = _out_mlp(x2, ys, wos, mlp_norm_w[layer].reshape(1, d), mlp_w1[layer].astype(BF16),
                      mlp_w2[layer].astype(BF16))
    return x2.reshape(bsz, t, d)
```

```python
import functools
import math
import struct

import jax
import jax.numpy as jnp
from jax import lax
from jax.experimental import pallas as pl
from jax.experimental.pallas import tpu as pltpu

F32 = jnp.float32
BF16 = jnp.bfloat16

LANES = 128
V7X_VMEM_LIMIT_BYTES = 56 * 1024 * 1024

EPS = 1e-6
CHUNK = 64
HEAD_DIM = 64
GROUP_TILE = 512
SSD_HEADS = 16
SSD_STATE = 128
SSD_GROUPS = 2
SSD_INNER = SSD_HEADS * HEAD_DIM
SSD_CONV = 4
SWA_HEADS = 16
SWA_KV_HEADS = 4
SWA_BLOCK = 128
DIFF_HEADS = 8
DIFF_BLOCK = 512
DIFF_STRIP = 32
DIFF_SCORE_SLOTS = 1
DIFF_STREAM_HEADROOM = 64.0
SSD_TILE = 256

NT_DIMS = (((1,), (1,)), ((), ()))
TN_DIMS = (((0,), (0,)), ((), ()))


def _alibi_slopes(n):
    return [2.0 ** (-8.0 * (i + 1) / n) for i in range(n)]


def _params(sem, vmem=V7X_VMEM_LIMIT_BYTES):
    return pltpu.CompilerParams(dimension_semantics=sem, vmem_limit_bytes=vmem)


def _rms_proj_kernel(*refs, norm_tiles, with_dt):
    if with_dt:
        x_ref, nw_ref, w_ref, gw_ref, nm_ref, g_ref, wd_ref, o_ref, dt_ref, h_ref = refs
    else:
        x_ref, nw_ref, w_ref, gw_ref, nm_ref, g_ref, o_ref, h_ref = refs
    tn = GROUP_TILE
    x = x_ref[...]
    ms = jnp.mean(x * x, axis=-1, keepdims=True)
    h_ref[...] = (x * lax.rsqrt(ms + EPS) * nw_ref[...]).astype(BF16)
    if with_dt:
        dt_ref[...] = jnp.dot(h_ref[...], wd_ref[...], preferred_element_type=F32)
    for j in range(w_ref.shape[1] // tn):
        cs = slice(j * tn, (j + 1) * tn)
        acc = jnp.dot(h_ref[...], w_ref[:, cs], preferred_element_type=F32)
        if j in norm_tiles:
            ss = jnp.dot((acc * acc).astype(BF16), g_ref[...], preferred_element_type=F32)
            inv = lax.rsqrt(ss * (1.0 / HEAD_DIM) + EPS)
            acc = acc * (jnp.where(nm_ref[:, cs] > 0.0, inv, 1.0) * gw_ref[:, cs])
        o_ref[:, cs] = acc.astype(o_ref.dtype)


def _resident(shape):
    return pl.BlockSpec(shape, lambda i: (0,) * len(shape), pipeline_mode=pl.Buffered(1))


def _rms_proj(x2d, norm_w, w, gw, nm, gmat, norm_tiles, wd=None, tm=512):
    m, d = x2d.shape
    n = w.shape[1]
    tm = min(tm, m)
    assert m % tm == 0 and n % GROUP_TILE == 0
    with_dt = wd is not None
    in_specs = [pl.BlockSpec((tm, d), lambda i: (i, 0)), _resident((1, d)), _resident((d, n)),
                _resident((1, n)), _resident((1, n)), _resident(gmat.shape)]
    args = [x2d, norm_w, w, gw, nm, gmat]
    out_shape = [jax.ShapeDtypeStruct((m, n), BF16)]
    out_specs = [pl.BlockSpec((tm, n), lambda i: (i, 0))]
    if with_dt:
        nd = wd.shape[1]
        in_specs.append(_resident((d, nd)))
        args.append(wd)
        out_shape.append(jax.ShapeDtypeStruct((m, nd), F32))
        out_specs.append(pl.BlockSpec((tm, nd), lambda i: (i, 0)))
    res = pl.pallas_call(
        functools.partial(_rms_proj_kernel, norm_tiles=tuple(norm_tiles), with_dt=with_dt),
        grid=(m // tm,),
        in_specs=in_specs,
        out_specs=out_specs,
        out_shape=out_shape,
        scratch_shapes=[pltpu.VMEM((tm, d), BF16)],
        compiler_params=_params(("arbitrary",)),
        name="rms_proj_dt" if with_dt else "rms_proj",
    )(*args)
    return res


def _out_mlp_kernel(*refs, n_mix, tf):
    x_ref = refs[0]
    y_refs = refs[1:1 + n_mix]
    wo_refs = refs[1 + n_mix:1 + 2 * n_mix]
    nw_ref, w1_ref, w2_ref, o_ref, hn_ref = refs[1 + 2 * n_mix:]
    x1 = x_ref[...]
    for y_ref, wo_ref in zip(y_refs, wo_refs):
        x1 = x1 + jnp.dot(y_ref[...], wo_ref[...], preferred_element_type=F32)
    ms = jnp.mean(x1 * x1, axis=-1, keepdims=True)
    hn_ref[...] = (x1 * lax.rsqrt(ms + EPS) * nw_ref[...]).astype(BF16)
    o_ref[...] = x1
    for f in range(w1_ref.shape[1] // tf):
        fs = slice(f * tf, (f + 1) * tf)
        hid = jnp.dot(hn_ref[...], w1_ref[:, fs], preferred_element_type=F32)
        act = jnp.square(jnp.maximum(hid, 0.0)).astype(BF16)
        o_ref[...] += jnp.dot(act, w2_ref[fs, :], preferred_element_type=F32)


def _out_mlp(x2d, ys, wos, norm_w, w1, w2, tm=512, tf=512):
    m, d = x2d.shape
    dff = w1.shape[1]
    tm = min(tm, m)
    assert m % tm == 0 and dff % tf == 0
    n_mix = len(ys)
    in_specs = [pl.BlockSpec((tm, d), lambda i: (i, 0))]
    in_specs += [pl.BlockSpec((tm, y.shape[1]), lambda i: (i, 0)) for y in ys]
    in_specs += [_resident(wo.shape) for wo in wos]
    in_specs += [_resident((1, d)), _resident(w1.shape), _resident(w2.shape)]
    return pl.pallas_call(
        functools.partial(_out_mlp_kernel, n_mix=n_mix, tf=tf),
        grid=(m // tm,),
        in_specs=in_specs,
        out_specs=pl.BlockSpec((tm, d), lambda i: (i, 0)),
        out_shape=jax.ShapeDtypeStruct((m, d), F32),
        scratch_shapes=[pltpu.VMEM((tm, d), BF16)],
        compiler_params=_params(("arbitrary",)),
        name="out_mlp",
    )(x2d, *ys, *wos, norm_w, w1, w2)


def _swa_kernel(sink_ref, q_ref, kp_ref, kc_ref, vp_ref, vc_ref, o_ref, bias_ref, s_ref, p_ref, rden_ref):
    n = pl.program_id(1)
    blk = SWA_BLOCK
    rows = lax.broadcasted_iota(jnp.int32, (blk, 2 * blk), 0)
    cols = lax.broadcasted_iota(jnp.int32, (blk, 2 * blk), 1)

    @pl.when(jnp.logical_and(pl.program_id(0) == 0, n == 0))
    def _():
        dist = jnp.abs(rows + blk - cols).astype(F32)
        dchunk = rows // CHUNK + 2 - cols // CHUNK
        valid = jnp.logical_and(dchunk >= 0, dchunk <= SWA_BLOCK // CHUNK)
        for h, slope in enumerate(_alibi_slopes(SWA_HEADS)):
            bias_ref[h * blk:(h + 1) * blk, :] = jnp.where(valid, (-LOG2E * slope) * dist, -jnp.inf)

    rep = SWA_HEADS // SWA_KV_HEADS
    first = jnp.where(jnp.logical_and(n == 0, cols < blk), -jnp.inf, 0.0)
    first = jnp.concatenate([first] * rep, axis=0)
    lane = lax.broadcasted_iota(jnp.int32, (1, LANES), 1)
    lo = lane < HEAD_DIM
    zero = jnp.zeros((), BF16)
    head_row = lax.broadcasted_iota(jnp.int32, (rep * blk, 1), 0) // blk

    v_both = []
    for kh in range(SWA_KV_HEADS):
        c0 = (kh // 2) * LANES
        k128 = jnp.concatenate([kp_ref[0, :, c0:c0 + LANES], kc_ref[0, :, c0:c0 + LANES]], axis=0)
        v128 = jnp.concatenate([vp_ref[0, :, c0:c0 + LANES], vc_ref[0, :, c0:c0 + LANES]], axis=0)
        k_sw = pltpu.roll(k128, HEAD_DIM, 1)
        v_sw = pltpu.roll(v128, HEAD_DIM, 1)
        k_both = jnp.where(lo, k128, k_sw) if kh % 2 == 0 else jnp.where(lo, k_sw, k128)
        v_both.append(jnp.where(lo, v128, v_sw) if kh % 2 == 0 else jnp.where(lo, v_sw, v128))
        parts = []
        for pr in range(rep // 2):
            pair = kh * (rep // 2) + pr
            q128 = q_ref[0, :, pair * LANES:(pair + 1) * LANES]
            parts += [jnp.where(lo, q128, zero), jnp.where(lo, zero, q128)]
        q4 = jnp.concatenate(parts, axis=0)
        s = lax.dot_general(q4, k_both, NT_DIMS, preferred_element_type=F32)
        s_ref[kh] = s + bias_ref[kh * rep * blk:(kh + 1) * rep * blk, :] + first
    for kh in range(SWA_KV_HEADS):
        sink = jnp.zeros((rep * blk, 1), F32)
        for r in range(rep):
            sink = jnp.where(head_row == r, sink_ref[kh * rep + r], sink)
        sink = sink * LOG2E
        s = s_ref[kh]
        m = jnp.maximum(jnp.max(s, axis=-1, keepdims=True), sink)
        ex = jnp.exp2(s - m)
        den = jnp.sum(ex, axis=-1, keepdims=True) + jnp.exp2(sink - m)
        p_ref[kh] = ex.astype(BF16)
        rden_ref[kh] = jnp.broadcast_to(1.0 / den, (rep * blk, LANES))
    for kh in range(SWA_KV_HEADS):
        o4 = jnp.dot(p_ref[kh], v_both[kh], preferred_element_type=F32) * rden_ref[kh]
        for pr in range(rep // 2):
            pair = kh * (rep // 2) + pr
            out = jnp.where(lo, o4[2 * pr * blk:(2 * pr + 1) * blk], o4[(2 * pr + 1) * blk:(2 * pr + 2) * blk])
            o_ref[0, :, pair * LANES:(pair + 1) * LANES] = out.astype(o_ref.dtype)


def _swa(proj3, sinks, q_col, k_col, v_col):
    b, t, _ = proj3.shape
    blk = SWA_BLOCK
    nb = t // blk
    dq = SWA_HEADS * HEAD_DIM
    dkv = SWA_KV_HEADS * HEAD_DIM
    qb, kb, vb = q_col // dq, k_col // dkv, v_col // dkv
    prev = lambda n: jnp.maximum(n - 1, 0)
    return pl.pallas_call(
        _swa_kernel,
        grid=(b, nb),
        in_specs=[
            pl.BlockSpec(memory_space=pltpu.SMEM),
            pl.BlockSpec((1, blk, dq), lambda i, n: (i, n, qb)),
            pl.BlockSpec((1, blk, dkv), lambda i, n: (i, prev(n), kb)),
            pl.BlockSpec((1, blk, dkv), lambda i, n: (i, n, kb)),
            pl.BlockSpec((1, blk, dkv), lambda i, n: (i, prev(n), vb)),
            pl.BlockSpec((1, blk, dkv), lambda i, n: (i, n, vb)),
        ],
        out_specs=pl.BlockSpec((1, blk, dq), lambda i, n: (i, n, 0)),
        out_shape=jax.ShapeDtypeStruct((b, t, dq), BF16),
        scratch_shapes=[
            pltpu.VMEM((SWA_HEADS * blk, 2 * blk), F32),
            pltpu.VMEM((SWA_KV_HEADS, SWA_HEADS // SWA_KV_HEADS * blk, 2 * blk), F32),
            pltpu.VMEM((SWA_KV_HEADS, SWA_HEADS // SWA_KV_HEADS * blk, 2 * blk), BF16),
            pltpu.VMEM((SWA_KV_HEADS, SWA_HEADS // SWA_KV_HEADS * blk, LANES), F32),
        ],
        compiler_params=_params(("arbitrary", "arbitrary")),
        name="swa",
    )(sinks, proj3, proj3, proj3, proj3, proj3)


def _split_bf16(a):
    hi = a.astype(BF16)
    lo = (a - hi.astype(F32)).astype(BF16)
    return hi, lo


def _ssd_kernel(z_ref, xs_ref, bc_ref, xsp_ref, bcp_ref, dt_ref, cw_ref, cb_ref, dtb_ref, alog_ref, dskip_ref,
                nw_ref, o_ref, st_ref):
    tt = SSD_TILE
    inner = SSD_INNER
    gw = inner // SSD_GROUPS
    ns = SSD_STATE
    t = pl.program_id(1)

    half = tt // 2

    @pl.when(t == 0)
    def _():
        st_ref[...] = jnp.zeros_like(st_ref)

    si = lax.broadcasted_iota(jnp.int32, ((SSD_CONV - 1) * half, tt), 0)
    sc = lax.broadcasted_iota(jnp.int32, ((SSD_CONV - 1) * half, tt), 1)
    sel = (sc == (si % half) + half - (SSD_CONV - 1) + si // half).astype(BF16)

    def conv_silu(cur_ref, prev_ref, c0, c1):
        outs = []
        for hh in range(2):
            cur = cur_ref[0, hh * half:(hh + 1) * half, :]
            if hh == 0:
                before = jnp.where(t > 0, prev_ref[0], jnp.zeros((), BF16))
            else:
                before = cur_ref[0, 0:half, :]
            sh = jnp.dot(sel, jnp.concatenate([before, cur], axis=0), preferred_element_type=F32)
            conv = cb_ref[:, c0:c1] + cw_ref[SSD_CONV - 1:SSD_CONV, c0:c1] * cur.astype(F32)
            for k in range(SSD_CONV - 1):
                conv = conv + cw_ref[k:k + 1, c0:c1] * sh[k * half:(k + 1) * half]
            outs.append(conv * jax.nn.sigmoid(conv))
        return jnp.concatenate(outs, axis=0)

    xs = conv_silu(xs_ref, xsp_ref, 0, inner)
    bcm = conv_silu(bc_ref, bcp_ref, inner, inner + 2 * SSD_GROUPS * ns).astype(BF16)
    bm = bcm[:, 0:SSD_GROUPS * ns]
    cm = bcm[:, SSD_GROUPS * ns:]

    dtx = dt_ref[0] + dtb_ref[...]
    dt = jnp.maximum(dtx, 0.0) + jnp.log(1.0 + jnp.exp(-jnp.abs(dtx)))
    a_dt = dt * (-jnp.exp(alog_ref[...]))

    ri = lax.broadcasted_iota(jnp.int32, (tt, tt), 0)
    ci = lax.broadcasted_iota(jnp.int32, (tt, tt), 1)
    tril = jnp.logical_and(ri // CHUNK == ci // CHUNK, ci <= ri).astype(BF16)
    rl = lax.broadcasted_iota(jnp.int32, (tt, 1), 0) % CHUNK
    cl = lax.broadcasted_iota(jnp.int32, (1, inner), 1) % HEAD_DIM

    hi, lo = _split_bf16(a_dt)
    a_cum = jnp.dot(tril, hi, preferred_element_type=F32) + jnp.dot(tril, lo, preferred_element_type=F32)
    hi, lo = _split_bf16(jnp.where(rl > cl, a_dt, 0.0))
    seg = jnp.dot(tril, hi, preferred_element_type=F32) + jnp.dot(tril, lo, preferred_element_type=F32)
    decay = jnp.exp(jnp.where(cl <= rl, seg, -jnp.inf))

    xdt = xs * dt
    quad = lax.broadcasted_iota(jnp.int32, (CHUNK, 4 * HEAD_DIM), 1) // HEAD_DIM
    zero = jnp.zeros((), BF16)

    for c in range(tt // CHUNK):
        r0 = c * CHUNK
        rs = slice(r0, r0 + CHUNK)
        acum_c = a_cum[rs]
        a_last = a_cum[r0 + CHUNK - 1:r0 + CHUNK]
        xdt_c = xdt[rs]
        xdt_b = xdt_c.astype(BF16)
        cbs = []
        for g in range(SSD_GROUPS):
            b_g = bm[rs, g * ns:(g + 1) * ns]
            c_g = cm[rs, g * ns:(g + 1) * ns]
            b_tiled = jnp.concatenate([b_g] * (gw // HEAD_DIM), axis=0)
            cbs.append(lax.dot_general(c_g, b_tiled, NT_DIMS, preferred_element_type=F32))
        gmat = (jnp.concatenate(cbs, axis=1) * decay[rs]).astype(BF16)
        yd = []
        for qd in range(inner // (4 * HEAD_DIM)):
            cs = slice(qd * 4 * HEAD_DIM, (qd + 1) * 4 * HEAD_DIM)
            xq = xdt_b[:, cs]
            xbd = jnp.concatenate([jnp.where(quad == jj, xq, zero) for jj in range(4)], axis=0)
            yd.append(jnp.dot(gmat[:, cs], xbd, preferred_element_type=F32))
        y = jnp.concatenate(yd, axis=1)
        din = jnp.exp(acum_c)
        x_end = (xdt_c * jnp.exp(a_last - acum_c)).astype(BF16)
        cdec = jnp.exp(a_last)
        yo = []
        for g in range(SSD_GROUPS):
            gs = slice(g * gw, (g + 1) * gw)
            b_g = bm[rs, g * ns:(g + 1) * ns]
            c_g = cm[rs, g * ns:(g + 1) * ns]
            st_g = st_ref[:, gs]
            yo.append(jnp.dot(c_g, st_g.astype(BF16), preferred_element_type=F32))
            upd = lax.dot_general(b_g, x_end[:, gs], TN_DIMS, preferred_element_type=F32)
            st_ref[:, gs] = st_g * cdec[:, gs] + upd
        y = y + jnp.concatenate(yo, axis=1) * din
        y = y + xs[rs] * dskip_ref[...]
        zz = z_ref[0, rs, :].astype(F32)
        y = y * (zz * jax.nn.sigmoid(zz))
        outs = []
        for g in range(SSD_GROUPS):
            y_g = y[:, g * gw:(g + 1) * gw]
            ms = jnp.mean(y_g * y_g, axis=-1, keepdims=True)
            outs.append(y_g * lax.rsqrt(ms + EPS))
        o_ref[0, rs, :] = (jnp.concatenate(outs, axis=1) * nw_ref[...]).astype(o_ref.dtype)


def _ssd(proj3, dtx3, conv_w, conv_b, dtb, alog, dskip, nw, z_col, xs_col, bc_col):
    b, t, _ = proj3.shape
    tt = SSD_TILE
    assert t % tt == 0
    inner = SSD_INNER
    bcw = 2 * SSD_GROUPS * SSD_STATE
    cdim = inner + bcw
    full = lambda shape: pl.BlockSpec(shape, lambda i, s: (0,) * len(shape))
    return pl.pallas_call(
        _ssd_kernel,
        grid=(b, t // tt),
        in_specs=[
            pl.BlockSpec((1, tt, inner), lambda i, s: (i, s, z_col // inner)),
            pl.BlockSpec((1, tt, inner), lambda i, s: (i, s, xs_col // inner)),
            pl.BlockSpec((1, tt, bcw), lambda i, s: (i, s, bc_col // bcw)),
            pl.BlockSpec((1, tt // 2, inner), lambda i, s: (i, jnp.maximum(2 * s - 1, 0), xs_col // inner)),
            pl.BlockSpec((1, tt // 2, bcw), lambda i, s: (i, jnp.maximum(2 * s - 1, 0), bc_col // bcw)),
            pl.BlockSpec((1, tt, inner), lambda i, s: (i, s, 0)),
            full((SSD_CONV, cdim)), full((1, cdim)), full((1, inner)), full((1, inner)),
            full((1, inner)), full((1, inner)),
        ],
        out_specs=pl.BlockSpec((1, tt, inner), lambda i, s: (i, s, 0)),
        out_shape=jax.ShapeDtypeStruct((b, t, inner), BF16),
        scratch_shapes=[pltpu.VMEM((SSD_STATE, inner), F32)],
        compiler_params=_params(("arbitrary", "arbitrary")),
        name="ssd",
    )(proj3, proj3, proj3, proj3, proj3, dtx3, conv_w, conv_b, dtb, alog, dskip, nw)


def _round_to_bf16(x):
    bits = struct.unpack("<I", struct.pack("<f", x))[0]
    bits = (bits + 0x7FFF + ((bits >> 16) & 1)) & 0xFFFF0000
    return struct.unpack("<f", struct.pack("<I", bits))[0]


def _bf16_terms(x, n):
    out = []
    for _ in range(n):
        out.append(_round_to_bf16(x))
        x -= out[-1]
    return out


LOG2E = math.log2(math.e)
LOG2E_TERMS = _bf16_terms(LOG2E, 3)
DIFF_BIAS_SPLIT = 16


def _diff_attn_kernel(slope_ref, lam_ref, q_ref, k_ref, v_ref, sn_ref, o_ref, tdiag_ref, kaug_ref, vt_ref,
                      *scratch, lam_init, nq):
    blk = DIFF_BLOCK
    strip = DIFF_STRIP
    ns = DIFF_SCORE_SLOTS
    qm_ref, p_ref, m_ref, l_ref, acc_ref = (scratch[2 * i:2 * i + 2] for i in range(5))
    s_ref = [scratch[10 + 2 * i:12 + 2 * i] for i in range(ns)]
    slope = jnp.full((1, 1), slope_ref[pl.program_id(1)], F32)
    kk =lax.broadcasted_iota(jnp.int32, (blk, blk), 0)
    qq = lax.broadcasted_iota(jnp.int32, (blk, blk), 1)
    tdiag_ref[...] = jnp.where(kk // CHUNK > qq // CHUNK, -jnp.inf,
                               jnp.where(kk > qq, (2.0 * LOG2E) * slope * (qq - kk).astype(F32), 0.0))

    lam_p = lam_ref[...]
    lam = (jnp.exp(jnp.sum(lam_p[0:1] * lam_p[1:2], axis=-1, keepdims=True))
           - jnp.exp(jnp.sum(lam_p[2:3] * lam_p[3:4], axis=-1, keepdims=True)) + lam_init)

    lane = lax.broadcasted_iota(jnp.int32, (blk, LANES), 1)
    pos = lax.broadcasted_iota(jnp.int32, (blk, LANES), 0)
    lo = lane < HEAD_DIM
    nt = len(LOG2E_TERMS)
    b_hi = slope * (DIFF_BIAS_SPLIT * (pos // DIFF_BIAS_SPLIT)).astype(F32)
    b_lo = slope * (pos % DIFF_BIAS_SPLIT).astype(F32)
    kbias, qconst = [], []
    for base in (HEAD_DIM, 0):
        rel = lane - base
        kbias.append(jnp.where((rel >= 0) & (rel < nt), b_hi,
                               jnp.where((rel >= nt) & (rel < 2 * nt), b_lo, 0.0)).astype(BF16))
        qc = jnp.zeros((blk, LANES), F32)
        for i, term in enumerate(LOG2E_TERMS):
            qc = jnp.where((rel == i) | (rel == nt + i), term, qc)
        qconst.append(qc.astype(BF16))
    for j in range(nq):
        k = k_ref[0, j * blk:(j + 1) * blk, :]
        kaug_ref[0, j] = jnp.where(lo, k, kbias[0])
        kaug_ref[1, j] = jnp.where(lo, kbias[1], k)
        vt_ref[j] = v_ref[0, j * blk:(j + 1) * blk, :].astype(F32).T.astype(BF16)

    def scores(slot, br, j, diag):
        s = lax.dot_general(kaug_ref[br, j], qm_ref[br][...], NT_DIMS, preferred_element_type=F32)
        if diag:
            s = s + tdiag_ref[...]
        s_ref[slot][br][...] = s
        return jnp.max(s, axis=0, keepdims=True)

    def exp_strips(br, src, off):
        lp = [None, None]
        for i, r in enumerate(range(0, blk, strip)):
            p = jnp.exp2(src[r:r + strip, :] - off)
            part = p[0:8]
            for g in range(8, strip, 8):
                part = part + p[g:g + 8]
            lp[i % 2] = part if lp[i % 2] is None else lp[i % 2] + part
            p_ref[br][r:r + strip, :] = p.astype(BF16)
        return lp[0] + lp[1]

    def absorb(slot, br, j, mx, shift):
        m_old = m_ref[br][0:1, :]
        m_new = jnp.maximum(m_old, mx + shift)
        alpha = jnp.exp2(m_old - m_new)
        m_ref[br][...] = jnp.broadcast_to(m_new, m_ref[br].shape)
        lp = exp_strips(br, s_ref[slot][br], m_new - shift)
        l_ref[br][...] = alpha * l_ref[br][...] + lp
        pv = jnp.dot(vt_ref[j], p_ref[br][...], preferred_element_type=F32)
        acc_ref[br][...] = alpha * acc_ref[br][...] + pv

    def raw_scores(br, j):
        return lax.dot_general(kaug_ref[br, j], qm_ref[br][...], NT_DIMS, preferred_element_type=F32)

    def absorb_streaming(br, j, s, shift):
        m_old = m_ref[br][0:1, :]
        excess = jnp.max(s, axis=0, keepdims=True) + shift - m_old
        lp = exp_strips(br, s, m_old - shift)
        pv = jnp.dot(vt_ref[j], p_ref[br][...], preferred_element_type=F32)
        up = jnp.maximum(excess, 0.0)
        alpha = jnp.exp2(-up)
        m_ref[br][...] = jnp.broadcast_to(m_old + up, m_ref[br].shape)
        l_ref[br][...] = alpha * (l_ref[br][...] + lp)
        acc_ref[br][...] = alpha * (acc_ref[br][...] + pv)
        return excess

    def q_step(qi, _):
        q0 = pl.multiple_of(qi * blk, blk)
        q = q_ref[0, pl.ds(q0, blk), :]
        qm_ref[0][...] = jnp.where(lo, q, qconst[0])
        qm_ref[1][...] = jnp.where(lo, qconst[1], q)

        def shift_of(j):
            return (-LOG2E * blk) * slope * jnp.full((1, 1), qi - j, jnp.int32).astype(F32)

        def reset(br, m_init):
            m_ref[br][...] = jnp.broadcast_to(m_init, m_ref[br].shape)
            l_ref[br][...] = jnp.zeros(l_ref[br].shape, F32)
            acc_ref[br][...] = jnp.zeros(acc_ref[br].shape, F32)

        zero_shift = jnp.zeros((1, 1), F32)
        s = []
        for br in range(2):
            prod = (kaug_ref[br, qi].astype(F32) * qm_ref[br][...].astype(F32)).astype(BF16)
            own = lax.dot_general(jnp.ones((8, LANES), BF16), prod, NT_DIMS, preferred_element_type=F32)
            reset(br, own[0:1, :])
            s.append(raw_scores(br, qi) + tdiag_ref[...])
        worst = jnp.maximum(absorb_streaming(0, qi, s[0], zero_shift), absorb_streaming(1, qi, s[1], zero_shift))

        def stream_step(j, worst):
            shift = shift_of(j)
            s = [raw_scores(br, j) for br in range(2)]
            e0 = absorb_streaming(0, j, s[0], shift)
            e1 = absorb_streaming(1, j, s[1], shift)
            return jnp.maximum(worst, jnp.maximum(e0, e1))

        worst = lax.fori_loop(0, qi // 2, lambda t, w: stream_step(2 * t + 1, stream_step(2 * t, w)), worst)
        worst = lax.cond(qi % 2 == 1, lambda w: stream_step(qi - 1, w), lambda w: w, worst)

        @pl.when(jnp.max(worst) > DIFF_STREAM_HEADROOM)
        def _():
            mx = [scores(0, br, qi, True) for br in range(2)]
            for br in range(2):
                reset(br, -jnp.inf)
                absorb(0, br, qi, mx[br], zero_shift)

            def two_pass_step(j, _):
                shift = shift_of(j)
                mx = [scores(0, br, j, False) for br in range(2)]
                for br in range(2):
                    absorb(0, br, j, mx[br], shift)
                return 0

            lax.fori_loop(0, qi, two_pass_step, 0)

        l1 = jnp.sum(l_ref[0][...], axis=0, keepdims=True)
        l2 = jnp.sum(l_ref[1][...], axis=0, keepdims=True)
        o = (acc_ref[0][...] / l1 - lam * (acc_ref[1][...] / l2)).T
        ms = jnp.mean(o * o, axis=-1, keepdims=True)
        o = o * lax.rsqrt(ms + EPS) * sn_ref[...] * (1.0 - lam_init)
        o_ref[0, pl.ds(q0, blk), :] = o.astype(o_ref.dtype)
        return 0

    lax.fori_loop(0, nq, q_step, 0)


def _diff_attn(proj3, lam_p, sub_norm, lam_init):
    b, t, _ = proj3.shape
    blk = DIFF_BLOCK
    assert t % blk == 0
    hd = 2 * HEAD_DIM
    nh = DIFF_HEADS
    assert all(math.log2(s).is_integer() for s in _alibi_slopes(nh))
    return pl.pallas_call(
        functools.partial(_diff_attn_kernel, lam_init=lam_init, nq=t // blk),
        grid=(b, nh),
        in_specs=[
            pl.BlockSpec(memory_space=pltpu.SMEM),
            pl.BlockSpec((4, HEAD_DIM), lambda i, h: (0, 0)),
            pl.BlockSpec((1, t, hd), lambda i, h: (i, 0, h)),
            pl.BlockSpec((1, t, hd), lambda i, h: (i, 0, nh + h)),
            pl.BlockSpec((1, t, hd), lambda i, h: (i, 0, 2 * nh + h)),
            pl.BlockSpec((1, hd), lambda i, h: (0, 0)),
        ],
        out_specs=pl.BlockSpec((1, t, hd), lambda i, h: (i, 0, h)),
        out_shape=jax.ShapeDtypeStruct((b, t, nh * hd), BF16),
        scratch_shapes=[
            pltpu.VMEM((blk, blk), F32), pltpu.VMEM((2, t // blk, blk, hd), BF16),
            pltpu.VMEM((t // blk, hd, blk), BF16),
            *[pltpu.VMEM((blk, hd), BF16)] * 2, *[pltpu.VMEM((blk, blk), BF16)] * 2,
            *[pltpu.VMEM((8, blk), F32)] * 2, *[pltpu.VMEM((8, blk), F32)] * 2, *[pltpu.VMEM((hd, blk), F32)] * 2,
            *[pltpu.VMEM((blk, blk), F32)] * (2 * DIFF_SCORE_SLOTS),
        ],
        compiler_params=_params(("arbitrary", "arbitrary")),
        name="diff_attn",
    )(jnp.asarray(_alibi_slopes(nh), F32), lam_p, proj3, proj3, proj3, sub_norm)


def _group_sum_matrix():
    idx = jnp.arange(GROUP_TILE) // HEAD_DIM
    return (idx[:, None] == idx[None, :]).astype(BF16)


def kernel(x, ev_norm_w, ev_w_in, ev_conv_w, ev_conv_b, ev_dt_bias, ev_a_log, ev_d_skip, ev_ssd_norm_w,
           ev_q_norm, ev_k_norm, ev_sinks, ev_w_out,
           od_norm_w, od_w_in, od_q_norm, od_k_norm, od_lam_q1, od_lam_k1, od_lam_q2, od_lam_k2,
           od_sub_norm, od_w_out, mlp_norm_w, mlp_w1, mlp_w2):
    bsz, t, d = x.shape
    m = bsz * t
    depth = mlp_w1.shape[0]
    gmat = _group_sum_matrix()
    x2 = x.reshape(m, d)
    for layer in range(depth):
        j = layer // 2
        if layer % 2 == 0:
            w_in = ev_w_in[j]
            inner = SSD_INNER
            bcw = 2 * SSD_GROUPS * SSD_STATE
            dq = SWA_HEADS * HEAD_DIM
            dkv = SWA_KV_HEADS * HEAD_DIM
            o_xbc = inner
            o_dt = o_xbc + inner + bcw
            o_q = o_dt + SSD_HEADS
            o_k = o_q + dq
            o_v = o_k + dkv
            w_re = jnp.concatenate([
                w_in[:, 0:inner], w_in[:, o_xbc:o_xbc + inner], w_in[:, o_q:o_q + dq],
                w_in[:, o_xbc + inner:o_dt], w_in[:, o_k:o_k + dkv], w_in[:, o_v:o_v + dkv]], axis=1).astype(BF16)
            z_col, xs_col, q_col, bc_col = 0, inner, 2 * inner, 2 * inner + dq
            k_col = bc_col + bcw
            v_col = k_col + dkv
            n_re = v_col + dkv
            w_dt = jnp.repeat(w_in[:, o_dt:o_dt + SSD_HEADS], HEAD_DIM, axis=1).astype(BF16)
            ones = lambda n: jnp.ones((n,), F32)
            gw = jnp.concatenate([ones(q_col), jnp.tile(ev_q_norm[j], SWA_HEADS) * (LOG2E * HEAD_DIM ** -0.5), ones(bcw),
                                  jnp.tile(ev_k_norm[j], SWA_KV_HEADS), ones(dkv)]).reshape(1, n_re)
            nm = jnp.concatenate([jnp.zeros((q_col,), F32), ones(dq), jnp.zeros((bcw,), F32), ones(dkv),
                                  jnp.zeros((dkv,), F32)]).reshape(1, n_re)
            norm_tiles = sorted({c // GROUP_TILE for c in list(range(q_col, q_col + dq, HEAD_DIM))
                                 + list(range(k_col, k_col + dkv, HEAD_DIM))})
            proj, dtx = _rms_proj(x2, ev_norm_w[j].reshape(1, d), w_re, gw, nm, gmat, norm_tiles, wd=w_dt)
            proj3 = proj.reshape(bsz, t, n_re)
            rep = lambda v: jnp.repeat(v, HEAD_DIM).reshape(1, inner)
            y_ssd = _ssd(proj3, dtx.reshape(bsz, t, inner), ev_conv_w[j], ev_conv_b[j].reshape(1, -1),
                         rep(ev_dt_bias[j]), rep(ev_a_log[j]), rep(ev_d_skip[j]),
                         ev_ssd_norm_w[j].reshape(1, inner), z_col, xs_col, bc_col)
            y_swa = _swa(proj3, ev_sinks[j], q_col, k_col, v_col)
            w_out = ev_w_out[j].astype(BF16)
            ys = [y_ssd.reshape(m, inner), y_swa.reshape(m, dq)]
            wos = [w_out[0:inner], w_out[inner:]]
        else:
            w_in = od_w_in[j].astype(BF16)
            dm = 2 * DIFF_HEADS * HEAD_DIM
            ones = jnp.ones((dm,), F32)
            gw = jnp.concatenate([jnp.tile(od_q_norm[j], 2 * DIFF_HEADS) * (LOG2E * HEAD_DIM ** -0.5),
                                  jnp.tile(od_k_norm[j], 2 * DIFF_HEADS), ones]).reshape(1, 3 * dm)
            nm = jnp.concatenate([ones, ones, jnp.zeros((dm,), F32)]).reshape(1, 3 * dm)
            norm_tiles = list(range(2 * dm // GROUP_TILE))
            (proj,) = _rms_proj(x2, od_norm_w[j].reshape(1, d), w_in, gw, nm, gmat, norm_tiles)
            lam_init = 0.8 - 0.6 * math.exp(-0.3 * layer)
            lam_p = jnp.stack([od_lam_q1[j], od_lam_k1[j], od_lam_q2[j], od_lam_k2[j]])
            y = _diff_attn(proj.reshape(bsz, t, 3 * dm), lam_p, od_sub_norm[j].reshape(1, -1), lam_init)
            ys = [y.reshape(m, dm)]
            wos = [od_w_out[j].astype(BF16)]
        x2 = _out_mlp(x2, ys, wos, mlp_norm_w[layer].reshape(1, d), mlp_w1[layer].astype(BF16),
                      mlp_w2[layer].astype(BF16))
    return x2.reshape(bsz, t, d)
```

```python
import functools
import math
import struct

import jax
import jax.numpy as jnp
from jax import lax
from jax.experimental import pallas as pl
from jax.experimental.pallas import tpu as pltpu

F32 = jnp.float32
BF16 = jnp.bfloat16

LANES = 128
V7X_VMEM_LIMIT_BYTES = 56 * 1024 * 1024

EPS = 1e-6
CHUNK = 64
HEAD_DIM = 64
GROUP_TILE = 512
SSD_HEADS = 16
SSD_STATE = 128
SSD_GROUPS = 2
SSD_INNER = SSD_HEADS * HEAD_DIM
SSD_CONV = 4
SWA_HEADS = 16
SWA_KV_HEADS = 4
SWA_BLOCK = 128
DIFF_HEADS = 8
DIFF_BLOCK = 512
DIFF_STRIP = 32
DIFF_SCORE_SLOTS = 1
DIFF_STREAM_HEADROOM = 64.0
SSD_TILE = 256

NT_DIMS = (((1,), (1,)), ((), ()))
TN_DIMS = (((0,), (0,)), ((), ()))


def _alibi_slopes(n):
    return [2.0 ** (-8.0 * (i + 1) / n) for i in range(n)]


def _params(sem, vmem=V7X_VMEM_LIMIT_BYTES):
    return pltpu.CompilerParams(dimension_semantics=sem, vmem_limit_bytes=vmem)


def _rms_proj_kernel(*refs, norm_tiles, with_dt):
    if with_dt:
        x_ref, nw_ref, w_ref, gw_ref, nm_ref, g_ref, wd_ref, o_ref, dt_ref, h_ref = refs
    else:
        x_ref, nw_ref, w_ref, gw_ref, nm_ref, g_ref, o_ref, h_ref = refs
    tn = GROUP_TILE
    x = x_ref[...]
    ms = jnp.mean(x * x, axis=-1, keepdims=True)
    h_ref[...] = (x * lax.rsqrt(ms + EPS) * nw_ref[...]).astype(BF16)
    if with_dt:
        dt_ref[...] = jnp.dot(h_ref[...], wd_ref[...], preferred_element_type=F32)
    for j in range(w_ref.shape[1] // tn):
        cs = slice(j * tn, (j + 1) * tn)
        acc = jnp.dot(h_ref[...], w_ref[:, cs], preferred_element_type=F32)
        if j in norm_tiles:
            ss = jnp.dot((acc * acc).astype(BF16), g_ref[...], preferred_element_type=F32)
            inv = lax.rsqrt(ss * (1.0 / HEAD_DIM) + EPS)
            acc = acc * (jnp.where(nm_ref[:, cs] > 0.0, inv, 1.0) * gw_ref[:, cs])
        o_ref[:, cs] = acc.astype(o_ref.dtype)


def _resident(shape):
    return pl.BlockSpec(shape, lambda i: (0,) * len(shape), pipeline_mode=pl.Buffered(1))


def _rms_proj(x2d, norm_w, w, gw, nm, gmat, norm_tiles, wd=None, tm=512):
    m, d = x2d.shape
    n = w.shape[1]
    tm = min(tm, m)
    assert m % tm == 0 and n % GROUP_TILE == 0
    with_dt = wd is not None
    in_specs = [pl.BlockSpec((tm, d), lambda i: (i, 0)), _resident((1, d)), _resident((d, n)),
                _resident((1, n)), _resident((1, n)), _resident(gmat.shape)]
    args = [x2d, norm_w, w, gw, nm, gmat]
    out_shape = [jax.ShapeDtypeStruct((m, n), BF16)]
    out_specs = [pl.BlockSpec((tm, n), lambda i: (i, 0))]
    if with_dt:
        nd = wd.shape[1]
        in_specs.append(_resident((d, nd)))
        args.append(wd)
        out_shape.append(jax.ShapeDtypeStruct((m, nd), F32))
        out_specs.append(pl.BlockSpec((tm, nd), lambda i: (i, 0)))
    res = pl.pallas_call(
        functools.partial(_rms_proj_kernel, norm_tiles=tuple(norm_tiles), with_dt=with_dt),
        grid=(m // tm,),
        in_specs=in_specs,
        out_specs=out_specs,
        out_shape=out_shape,
        scratch_shapes=[pltpu.VMEM((tm, d), BF16)],
        compiler_params=_params(("arbitrary",)),
        name="rms_proj_dt" if with_dt else "rms_proj",
    )(*args)
    return res


def _out_mlp_kernel(*refs, n_mix, tf):
    x_ref = refs[0]
    y_refs = refs[1:1 + n_mix]
    wo_refs = refs[1 + n_mix:1 + 2 * n_mix]
    nw_ref, w1_ref, w2_ref, o_ref, hn_ref = refs[1 + 2 * n_mix:]
    x1 = x_ref[...]
    for y_ref, wo_ref in zip(y_refs, wo_refs):
        x1 = x1 + jnp.dot(y_ref[...], wo_ref[...], preferred_element_type=F32)
    ms = jnp.mean(x1 * x1, axis=-1, keepdims=True)
    hn_ref[...] = (x1 * lax.rsqrt(ms + EPS) * nw_ref[...]).astype(BF16)
    o_ref[...] = x1
    for f in range(w1_ref.shape[1] // tf):
        fs = slice(f * tf, (f + 1) * tf)
        hid = jnp.dot(hn_ref[...], w1_ref[:, fs], preferred_element_type=F32)
        act = jnp.square(jnp.maximum(hid, 0.0)).astype(BF16)
        o_ref[...] += jnp.dot(act, w2_ref[fs, :], preferred_element_type=F32)


def _out_mlp(x2d, ys, wos, norm_w, w1, w2, tm=512, tf=512):
    m, d = x2d.shape
    dff = w1.shape[1]
    tm = min(tm, m)
    assert m % tm == 0 and dff % tf == 0
    n_mix = len(ys)
    in_specs = [pl.BlockSpec((tm, d), lambda i: (i, 0))]
    in_specs += [pl.BlockSpec((tm, y.shape[1]), lambda i: (i, 0)) for y in ys]
    in_specs += [_resident(wo.shape) for wo in wos]
    in_specs += [_resident((1, d)), _resident(w1.shape), _resident(w2.shape)]
    return pl.pallas_call(
        functools.partial(_out_mlp_kernel, n_mix=n_mix, tf=tf),
        grid=(m // tm,),
        in_specs=in_specs,
        out_specs=pl.BlockSpec((tm, d), lambda i: (i, 0)),
        out_shape=jax.ShapeDtypeStruct((m, d), F32),
        scratch_shapes=[pltpu.VMEM((tm, d), BF16)],
        compiler_params=_params(("arbitrary",)),
        name="out_mlp",
    )(x2d, *ys, *wos, norm_w, w1, w2)


def _swa_kernel(sink_ref, q_ref, kp_ref, kc_ref, vp_ref, vc_ref, o_ref, bias_ref, s_ref, p_ref, rden_ref):
    n = pl.program_id(1)
    blk = SWA_BLOCK
    rows = lax.broadcasted_iota(jnp.int32, (blk, 2 * blk), 0)
    cols = lax.broadcasted_iota(jnp.int32, (blk, 2 * blk), 1)

    @pl.when(jnp.logical_and(pl.program_id(0) == 0, n == 0))
    def _():
        dist = jnp.abs(rows + blk - cols).astype(F32)
        dchunk = rows // CHUNK + 2 - cols // CHUNK
        valid = jnp.logical_and(dchunk >= 0, dchunk <= SWA_BLOCK // CHUNK)
        for h, slope in enumerate(_alibi_slopes(SWA_HEADS)):
            bias_ref[h * blk:(h + 1) * blk, :] = jnp.where(valid, (-LOG2E * slope) * dist, -jnp.inf)

    rep = SWA_HEADS // SWA_KV_HEADS
    first = jnp.where(jnp.logical_and(n == 0, cols < blk), -jnp.inf, 0.0)
    first = jnp.concatenate([first] * rep, axis=0)
    lane = lax.broadcasted_iota(jnp.int32, (1, LANES), 1)
    lo = lane < HEAD_DIM
    zero = jnp.zeros((), BF16)
    head_row = lax.broadcasted_iota(jnp.int32, (rep * blk, 1), 0) // blk

    v_both = []
    for kh in range(SWA_KV_HEADS):
        c0 = (kh // 2) * LANES
        k128 = jnp.concatenate([kp_ref[0, :, c0:c0 + LANES], kc_ref[0, :, c0:c0 + LANES]], axis=0)
        v128 = jnp.concatenate([vp_ref[0, :, c0:c0 + LANES], vc_ref[0, :, c0:c0 + LANES]], axis=0)
        k_sw = pltpu.roll(k128, HEAD_DIM, 1)
        v_sw = pltpu.roll(v128, HEAD_DIM, 1)
        k_both = jnp.where(lo, k128, k_sw) if kh % 2 == 0 else jnp.where(lo, k_sw, k128)
        v_both.append(jnp.where(lo, v128, v_sw) if kh % 2 == 0 else jnp.where(lo, v_sw, v128))
        parts = []
        for pr in range(rep // 2):
            pair = kh * (rep // 2) + pr
            q128 = q_ref[0, :, pair * LANES:(pair + 1) * LANES]
            parts += [jnp.where(lo, q128, zero), jnp.where(lo, zero, q128)]
        q4 = jnp.concatenate(parts, axis=0)
        s = lax.dot_general(q4, k_both, NT_DIMS, preferred_element_type=F32)
        s_ref[kh] = s + bias_ref[kh * rep * blk:(kh + 1) * rep * blk, :] + first
    for kh in range(SWA_KV_HEADS):
        sink = jnp.zeros((rep * blk, 1), F32)
        for r in range(rep):
            sink = jnp.where(head_row == r, sink_ref[kh * rep + r], sink)
        sink = sink * LOG2E
        s = s_ref[kh]
        m = jnp.maximum(jnp.max(s, axis=-1, keepdims=True), sink)
        ex = jnp.exp2(s - m)
        den = jnp.sum(ex, axis=-1, keepdims=True) + jnp.exp2(sink - m)
        p_ref[kh] = ex.astype(BF16)
        rden_ref[kh] = jnp.broadcast_to(1.0 / den, (rep * blk, LANES))
    for kh in range(SWA_KV_HEADS):
        o4 = jnp.dot(p_ref[kh], v_both[kh], preferred_element_type=F32) * rden_ref[kh]
        for pr in range(rep // 2):
            pair = kh * (rep // 2) + pr
            out = jnp.where(lo, o4[2 * pr * blk:(2 * pr + 1) * blk], o4[(2 * pr + 1) * blk:(2 * pr + 2) * blk])
            o_ref[0, :, pair * LANES:(pair + 1) * LANES] = out.astype(o_ref.dtype)


def _swa(proj3, sinks, q_col, k_col, v_col):
    b, t, _ = proj3.shape
    blk = SWA_BLOCK
    nb = t // blk
    dq = SWA_HEADS * HEAD_DIM
    dkv = SWA_KV_HEADS * HEAD_DIM
    qb, kb, vb = q_col // dq, k_col // dkv, v_col // dkv
    prev = lambda n: jnp.maximum(n - 1, 0)
    return pl.pallas_call(
        _swa_kernel,
        grid=(b, nb),
        in_specs=[
            pl.BlockSpec(memory_space=pltpu.SMEM),
            pl.BlockSpec((1, blk, dq), lambda i, n: (i, n, qb)),
            pl.BlockSpec((1, blk, dkv), lambda i, n: (i, prev(n), kb)),
            pl.BlockSpec((1, blk, dkv), lambda i, n: (i, n, kb)),
            pl.BlockSpec((1, blk, dkv), lambda i, n: (i, prev(n), vb)),
            pl.BlockSpec((1, blk, dkv), lambda i, n: (i, n, vb)),
        ],
        out_specs=pl.BlockSpec((1, blk, dq), lambda i, n: (i, n, 0)),
        out_shape=jax.ShapeDtypeStruct((b, t, dq), BF16),
        scratch_shapes=[
            pltpu.VMEM((SWA_HEADS * blk, 2 * blk), F32),
            pltpu.VMEM((SWA_KV_HEADS, SWA_HEADS // SWA_KV_HEADS * blk, 2 * blk), F32),
            pltpu.VMEM((SWA_KV_HEADS, SWA_HEADS // SWA_KV_HEADS * blk, 2 * blk), BF16),
            pltpu.VMEM((SWA_KV_HEADS, SWA_HEADS // SWA_KV_HEADS * blk, LANES), F32),
        ],
        compiler_params=_params(("arbitrary", "arbitrary")),
        name="swa",
    )(sinks, proj3, proj3, proj3, proj3, proj3)


def _split_bf16(a):
    hi = a.astype(BF16)
    lo = (a - hi.astype(F32)).astype(BF16)
    return hi, lo


def _ssd_kernel(z_ref, xs_ref, bc_ref, xsp_ref, bcp_ref, dt_ref, cw_ref, cb_ref, dtb_ref, alog_ref, dskip_ref,
                nw_ref, o_ref, st_ref):
    tt = SSD_TILE
    inner = SSD_INNER
    gw = inner // SSD_GROUPS
    ns = SSD_STATE
    t = pl.program_id(1)

    half = tt // 2

    @pl.when(t == 0)
    def _():
        st_ref[...] = jnp.zeros_like(st_ref)

    si = lax.broadcasted_iota(jnp.int32, ((SSD_CONV - 1) * half, tt), 0)
    sc = lax.broadcasted_iota(jnp.int32, ((SSD_CONV - 1) * half, tt), 1)
    sel = (sc == (si % half) + half - (SSD_CONV - 1) + si // half).astype(BF16)

    def conv_silu(cur_ref, prev_ref, c0, c1):
        outs = []
        for hh in range(2):
            cur = cur_ref[0, hh * half:(hh + 1) * half, :]
            if hh == 0:
                before = jnp.where(t > 0, prev_ref[0], jnp.zeros((), BF16))
            else:
                before = cur_ref[0, 0:half, :]
            sh = jnp.dot(sel, jnp.concatenate([before, cur], axis=0), preferred_element_type=F32)
            conv = cb_ref[:, c0:c1] + cw_ref[SSD_CONV - 1:SSD_CONV, c0:c1] * cur.astype(F32)
            for k in range(SSD_CONV - 1):
                conv = conv + cw_ref[k:k + 1, c0:c1] * sh[k * half:(k + 1) * half]
            outs.append(conv * jax.nn.sigmoid(conv))
        return jnp.concatenate(outs, axis=0)

    xs = conv_silu(xs_ref, xsp_ref, 0, inner)
    bcm = conv_silu(bc_ref, bcp_ref, inner, inner + 2 * SSD_GROUPS * ns).astype(BF16)
    bm = bcm[:, 0:SSD_GROUPS * ns]
    cm = bcm[:, SSD_GROUPS * ns:]

    dtx = dt_ref[0] + dtb_ref[...]
    dt = jnp.maximum(dtx, 0.0) + jnp.log(1.0 + jnp.exp(-jnp.abs(dtx)))
    a_dt = dt * (-jnp.exp(alog_ref[...]))

    ri = lax.broadcasted_iota(jnp.int32, (tt, tt), 0)
    ci = lax.broadcasted_iota(jnp.int32, (tt, tt), 1)
    tril = jnp.logical_and(ri // CHUNK == ci // CHUNK, ci <= ri).astype(BF16)
    rl = lax.broadcasted_iota(jnp.int32, (tt, 1), 0) % CHUNK
    cl = lax.broadcasted_iota(jnp.int32, (1, inner), 1) % HEAD_DIM

    hi, lo = _split_bf16(a_dt)
    a_cum = jnp.dot(tril, hi, preferred_element_type=F32) + jnp.dot(tril, lo, preferred_element_type=F32)
    hi, lo = _split_bf16(jnp.where(rl > cl, a_dt, 0.0))
    seg = jnp.dot(tril, hi, preferred_element_type=F32) + jnp.dot(tril, lo, preferred_element_type=F32)
    decay = jnp.exp(jnp.where(cl <= rl, seg, -jnp.inf))

    xdt = xs * dt
    quad = lax.broadcasted_iota(jnp.int32, (CHUNK, 4 * HEAD_DIM), 1) // HEAD_DIM
    zero = jnp.zeros((), BF16)

    for c in range(tt // CHUNK):
        r0 = c * CHUNK
        rs = slice(r0, r0 + CHUNK)
        acum_c = a_cum[rs]
        a_last = a_cum[r0 + CHUNK - 1:r0 + CHUNK]
        xdt_c = xdt[rs]
        xdt_b = xdt_c.astype(BF16)
        cbs = []
        for g in range(SSD_GROUPS):
            b_g = bm[rs, g * ns:(g + 1) * ns]
            c_g = cm[rs, g * ns:(g + 1) * ns]
            b_tiled = jnp.concatenate([b_g] * (gw // HEAD_DIM), axis=0)
            cbs.append(lax.dot_general(c_g, b_tiled, NT_DIMS, preferred_element_type=F32))
        gmat = (jnp.concatenate(cbs, axis=1) * decay[rs]).astype(BF16)
        yd = []
        for qd in range(inner // (4 * HEAD_DIM)):
            cs = slice(qd * 4 * HEAD_DIM, (qd + 1) * 4 * HEAD_DIM)
            xq = xdt_b[:, cs]
            xbd = jnp.concatenate([jnp.where(quad == jj, xq, zero) for jj in range(4)], axis=0)
            yd.append(jnp.dot(gmat[:, cs], xbd, preferred_element_type=F32))
        y = jnp.concatenate(yd, axis=1)
        din = jnp.exp(acum_c)
        x_end = (xdt_c * jnp.exp(a_last - acum_c)).astype(BF16)
        cdec = jnp.exp(a_last)
        yo = []
        for g in range(SSD_GROUPS):
            gs = slice(g * gw, (g + 1) * gw)
            b_g = bm[rs, g * ns:(g + 1) * ns]
            c_g = cm[rs, g * ns:(g + 1) * ns]
            st_g = st_ref[:, gs]
            yo.append(jnp.dot(c_g, st_g.astype(BF16), preferred_element_type=F32))
            upd = lax.dot_general(b_g, x_end[:, gs], TN_DIMS, preferred_element_type=F32)
            st_ref[:, gs] = st_g * cdec[:, gs] + upd
        y = y + jnp.concatenate(yo, axis=1) * din
        y = y + xs[rs] * dskip_ref[...]
        zz = z_ref[0, rs, :].astype(F32)
        y = y * (zz * jax.nn.sigmoid(zz))
        outs = []
        for g in range(SSD_GROUPS):
            y_g = y[:, g * gw:(g + 1) * gw]
            ms = jnp.mean(y_g * y_g, axis=-1, keepdims=True)
            outs.append(y_g * lax.rsqrt(ms + EPS))
        o_ref[0, rs, :] = (jnp.concatenate(outs, axis=1) * nw_ref[...]).astype(o_ref.dtype)


def _ssd(proj3, dtx3, conv_w, conv_b, dtb, alog, dskip, nw, z_col, xs_col, bc_col):
    b, t, _ = proj3.shape
    tt = SSD_TILE
    assert t % tt == 0
    inner = SSD_INNER
    bcw = 2 * SSD_GROUPS * SSD_STATE
    cdim = inner + bcw
    full = lambda shape: pl.BlockSpec(shape, lambda i, s: (0,) * len(shape))
    return pl.pallas_call(
        _ssd_kernel,
        grid=(b, t // tt),
        in_specs=[
            pl.BlockSpec((1, tt, inner), lambda i, s: (i, s, z_col // inner)),
            pl.BlockSpec((1, tt, inner), lambda i, s: (i, s, xs_col // inner)),
            pl.BlockSpec((1, tt, bcw), lambda i, s: (i, s, bc_col // bcw)),
            pl.BlockSpec((1, tt // 2, inner), lambda i, s: (i, jnp.maximum(2 * s - 1, 0), xs_col // inner)),
            pl.BlockSpec((1, tt // 2, bcw), lambda i, s: (i, jnp.maximum(2 * s - 1, 0), bc_col // bcw)),
            pl.BlockSpec((1, tt, inner), lambda i, s: (i, s, 0)),
            full((SSD_CONV, cdim)), full((1, cdim)), full((1, inner)), full((1, inner)),
            full((1, inner)), full((1, inner)),
        ],
        out_specs=pl.BlockSpec((1, tt, inner), lambda i, s: (i, s, 0)),
        out_shape=jax.ShapeDtypeStruct((b, t, inner), BF16),
        scratch_shapes=[pltpu.VMEM((SSD_STATE, inner), F32)],
        compiler_params=_params(("arbitrary", "arbitrary")),
        name="ssd",
    )(proj3, proj3, proj3, proj3, proj3, dtx3, conv_w, conv_b, dtb, alog, dskip, nw)


def _round_to_bf16(x):
    bits = struct.unpack("<I", struct.pack("<f", x))[0]
    bits = (bits + 0x7FFF + ((bits >> 16) & 1)) & 0xFFFF0000
    return struct.unpack("<f", struct.pack("<I", bits))[0]


def _bf16_terms(x, n):
    out = []
    for _ in range(n):
        out.append(_round_to_bf16(x))
        x -= out[-1]
    return out


LOG2E = math.log2(math.e)
LOG2E_TERMS = _bf16_terms(LOG2E, 3)
DIFF_BIAS_SPLIT = 16


def _diff_attn_kernel(slope_ref, lam_ref, q_ref, k_ref, v_ref, sn_ref, o_ref, tdiag_ref, kaug_ref, vt_ref,
                      *scratch, lam_init, nq):
    blk = DIFF_BLOCK
    strip = DIFF_STRIP
    ns = DIFF_SCORE_SLOTS
    qm_ref, p_ref, m_ref, l_ref, acc_ref, lfin_ref, accfin_ref = (scratch[2 * i:2 * i + 2] for i in range(7))
    s_ref = [scratch[14 + 2 * i:16 + 2 * i] for i in range(ns)]
    slope = jnp.full((1, 1), slope_ref[pl.program_id(1)], F32)
    kk =lax.broadcasted_iota(jnp.int32, (blk, blk), 0)
    qq = lax.broadcasted_iota(jnp.int32, (blk, blk), 1)
    tdiag_ref[...] = jnp.where(kk // CHUNK > qq // CHUNK, -jnp.inf,
                               jnp.where(kk > qq, (2.0 * LOG2E) * slope * (qq - kk).astype(F32), 0.0))

    lam_p = lam_ref[...]
    lam = (jnp.exp(jnp.sum(lam_p[0:1] * lam_p[1:2], axis=-1, keepdims=True))
           - jnp.exp(jnp.sum(lam_p[2:3] * lam_p[3:4], axis=-1, keepdims=True)) + lam_init)

    lane = lax.broadcasted_iota(jnp.int32, (blk, LANES), 1)
    pos = lax.broadcasted_iota(jnp.int32, (blk, LANES), 0)
    lo = lane < HEAD_DIM
    nt = len(LOG2E_TERMS)
    b_hi = slope * (DIFF_BIAS_SPLIT * (pos // DIFF_BIAS_SPLIT)).astype(F32)
    b_lo = slope * (pos % DIFF_BIAS_SPLIT).astype(F32)
    kbias, qconst = [], []
    for base in (HEAD_DIM, 0):
        rel = lane - base
        kbias.append(jnp.where((rel >= 0) & (rel < nt), b_hi,
                               jnp.where((rel >= nt) & (rel < 2 * nt), b_lo, 0.0)).astype(BF16))
        qc = jnp.zeros((blk, LANES), F32)
        for i, term in enumerate(LOG2E_TERMS):
            qc = jnp.where((rel == i) | (rel == nt + i), term, qc)
        qconst.append(qc.astype(BF16))
    for j in range(nq):
        k = k_ref[0, j * blk:(j + 1) * blk, :]
        kaug_ref[0, j] = jnp.where(lo, k, kbias[0])
        kaug_ref[1, j] = jnp.where(lo, kbias[1], k)
        vt_ref[j] = v_ref[0, j * blk:(j + 1) * blk, :].astype(F32).T.astype(BF16)

    def scores(slot, br, j, diag):
        s = lax.dot_general(kaug_ref[br, j], qm_ref[br][...], NT_DIMS, preferred_element_type=F32)
        if diag:
            s = s + tdiag_ref[...]
        s_ref[slot][br][...] = s
        return jnp.max(s, axis=0, keepdims=True)

    def exp_strips(br, src, off):
        lp = [None, None]
        for i, r in enumerate(range(0, blk, strip)):
            p = jnp.exp2(src[r:r + strip, :] - off)
            part = p[0:8]
            for g in range(8, strip, 8):
                part = part + p[g:g + 8]
            lp[i % 2] = part if lp[i % 2] is None else lp[i % 2] + part
            p_ref[br][r:r + strip, :] = p.astype(BF16)
        return lp[0] + lp[1]

    def absorb(slot, br, j, mx, shift):
        m_old = m_ref[br][0:1, :]
        m_new = jnp.maximum(m_old, mx + shift)
        alpha = jnp.exp2(m_old - m_new)
        m_ref[br][...] = jnp.broadcast_to(m_new, m_ref[br].shape)
        lp = exp_strips(br, s_ref[slot][br], m_new - shift)
        l_ref[br][...] = alpha * l_ref[br][...] + lp
        pv = jnp.dot(vt_ref[j], p_ref[br][...], preferred_element_type=F32)
        acc_ref[br][...] = alpha * acc_ref[br][...] + pv

    def raw_scores(br, j):
        return lax.dot_general(kaug_ref[br, j], qm_ref[br][...], NT_DIMS, preferred_element_type=F32)

    def absorb_streaming(br, j, s, shift):
        m_old = m_ref[br][0:1, :]
        excess = jnp.max(s, axis=0, keepdims=True) + shift - m_old
        lp = exp_strips(br, s, m_old - shift)
        pv = jnp.dot(vt_ref[j], p_ref[br][...], preferred_element_type=F32)
        up = jnp.maximum(excess, 0.0)
        alpha = jnp.exp2(-up)
        m_ref[br][...] = jnp.broadcast_to(m_old + up, m_ref[br].shape)
        l_ref[br][...] = alpha * (l_ref[br][...] + lp)
        acc_ref[br][...] = alpha * (acc_ref[br][...] + pv)
        return excess

    zero_shift = jnp.zeros((1, 1), F32)

    def shift_of(qi, j):
        return (-LOG2E * blk) * slope * jnp.full((1, 1), qi - j, jnp.int32).astype(F32)

    def reset(br, m_init):
        m_ref[br][...] = jnp.broadcast_to(m_init, m_ref[br].shape)
        l_ref[br][...] = jnp.zeros(l_ref[br].shape, F32)
        acc_ref[br][...] = jnp.zeros(acc_ref[br].shape, F32)

    def load_queries(qi):
        q = q_ref[0, pl.ds(pl.multiple_of(qi * blk, blk), blk), :]
        qm_ref[0][...] = jnp.where(lo, q, qconst[0])
        qm_ref[1][...] = jnp.where(lo, qconst[1], q)

    def write_output(qi, l_src, acc_src):
        r1 = 1.0 / jnp.sum(l_src[0][...], axis=0, keepdims=True)
        r2 = lam / jnp.sum(l_src[1][...], axis=0, keepdims=True)
        o = (acc_src[0][...] * r1 - acc_src[1][...] * r2).T
        ms = jnp.mean(o * o, axis=-1, keepdims=True)
        o = o * lax.rsqrt(ms + EPS) * sn_ref[...] * (1.0 - lam_init)
        q0 = qi * blk if isinstance(qi, int) else pl.multiple_of(qi * blk, blk)
        o_ref[0, pl.ds(q0, blk), :] = o.astype(o_ref.dtype)

    for br in range(2):
        lfin_ref[br][...] = jnp.ones(lfin_ref[br].shape, F32)
        accfin_ref[br][...] = jnp.zeros(accfin_ref[br].shape, F32)

    def q_step(qi, worst):
        write_output(jnp.maximum(qi - 1, 0), lfin_ref, accfin_ref)
        load_queries(qi)
        s = []
        for br in range(2):
            prod = (kaug_ref[br, qi].astype(F32) * qm_ref[br][...].astype(F32)).astype(BF16)
            own = lax.dot_general(jnp.ones((8, LANES), BF16), prod, NT_DIMS, preferred_element_type=F32)
            reset(br, own[0:1, :])
            s.append(raw_scores(br, qi) + tdiag_ref[...])
        worst = jnp.maximum(worst, jnp.maximum(absorb_streaming(0, qi, s[0], zero_shift),
                                               absorb_streaming(1, qi, s[1], zero_shift)))

        def stream_step(j, worst):
            shift = shift_of(qi, j)
            s = [raw_scores(br, j) for br in range(2)]
            e0 = absorb_streaming(0, j, s[0], shift)
            e1 = absorb_streaming(1, j, s[1], shift)
            return jnp.maximum(worst, jnp.maximum(e0, e1))

        worst = lax.fori_loop(0, qi // 2, lambda t, w: stream_step(2 * t + 1, stream_step(2 * t, w)), worst)
        worst = lax.cond(qi % 2 == 1, lambda w: stream_step(qi - 1, w), lambda w: w, worst)
        for br in range(2):
            lfin_ref[br][...] = l_ref[br][...]
            accfin_ref[br][...] = acc_ref[br][...]
        return worst

    worst = lax.fori_loop(0, nq, q_step, jnp.full((1, blk), -jnp.inf, F32))
    write_output(nq - 1, lfin_ref, accfin_ref)

    @pl.when(jnp.max(worst) > DIFF_STREAM_HEADROOM)
    def _():
        def q_step_two_pass(qi, _):
            load_queries(qi)
            mx = [scores(0, br, qi, True) for br in range(2)]
            for br in range(2):
                reset(br, -jnp.inf)
                absorb(0, br, qi, mx[br], zero_shift)

            def kv_step(j, _):
                mx = [scores(0, br, j, False) for br in range(2)]
                for br in range(2):
                    absorb(0, br, j, mx[br], shift_of(qi, j))
                return 0

            lax.fori_loop(0, qi, kv_step, 0)
            write_output(qi, l_ref, acc_ref)
            return 0

        lax.fori_loop(0, nq, q_step_two_pass, 0)


def _diff_attn(proj3, lam_p, sub_norm, lam_init):
    b, t, _ = proj3.shape
    blk = DIFF_BLOCK
    assert t % blk == 0
    hd = 2 * HEAD_DIM
    nh = DIFF_HEADS
    assert all(math.log2(s).is_integer() for s in _alibi_slopes(nh))
    return pl.pallas_call(
        functools.partial(_diff_attn_kernel, lam_init=lam_init, nq=t // blk),
        grid=(b, nh),
        in_specs=[
            pl.BlockSpec(memory_space=pltpu.SMEM),
            pl.BlockSpec((4, HEAD_DIM), lambda i, h: (0, 0)),
            pl.BlockSpec((1, t, hd), lambda i, h: (i, 0, h)),
            pl.BlockSpec((1, t, hd), lambda i, h: (i, 0, nh + h)),
            pl.BlockSpec((1, t, hd), lambda i, h: (i, 0, 2 * nh + h)),
            pl.BlockSpec((1, hd), lambda i, h: (0, 0)),
        ],
        out_specs=pl.BlockSpec((1, t, hd), lambda i, h: (i, 0, h)),
        out_shape=jax.ShapeDtypeStruct((b, t, nh * hd), BF16),
        scratch_shapes=[
            pltpu.VMEM((blk, blk), F32), pltpu.VMEM((2, t // blk, blk, hd), BF16),
            pltpu.VMEM((t // blk, hd, blk), BF16),
            *[pltpu.VMEM((blk, hd), BF16)] * 2, *[pltpu.VMEM((blk, blk), BF16)] * 2,
            *[pltpu.VMEM((8, blk), F32)] * 2, *[pltpu.VMEM((8, blk), F32)] * 2, *[pltpu.VMEM((hd, blk), F32)] * 2,
            *[pltpu.VMEM((8, blk), F32)] * 2, *[pltpu.VMEM((hd, blk), F32)] * 2,
            *[pltpu.VMEM((blk, blk), F32)] * (2 * DIFF_SCORE_SLOTS),
        ],
        compiler_params=_params(("arbitrary", "arbitrary")),
        name="diff_attn",
    )(jnp.asarray(_alibi_slopes(nh), F32), lam_p, proj3, proj3, proj3, sub_norm)


def _group_sum_matrix():
    idx = jnp.arange(GROUP_TILE) // HEAD_DIM
    return (idx[:, None] == idx[None, :]).astype(BF16)


def kernel(x, ev_norm_w, ev_w_in, ev_conv_w, ev_conv_b, ev_dt_bias, ev_a_log, ev_d_skip, ev_ssd_norm_w,
           ev_q_norm, ev_k_norm, ev_sinks, ev_w_out,
           od_norm_w, od_w_in, od_q_norm, od_k_norm, od_lam_q1, od_lam_k1, od_lam_q2, od_lam_k2,
           od_sub_norm, od_w_out, mlp_norm_w, mlp_w1, mlp_w2):
    bsz, t, d = x.shape
    m = bsz * t
    depth = mlp_w1.shape[0]
    gmat = _group_sum_matrix()
    x2 = x.reshape(m, d)
    for layer in range(depth):
        j = layer // 2
        if layer % 2 == 0:
            w_in = ev_w_in[j]
            inner = SSD_INNER
            bcw = 2 * SSD_GROUPS * SSD_STATE
            dq = SWA_HEADS * HEAD_DIM
            dkv = SWA_KV_HEADS * HEAD_DIM
            o_xbc = inner
            o_dt = o_xbc + inner + bcw
            o_q = o_dt + SSD_HEADS
            o_k = o_q + dq
            o_v = o_k + dkv
            w_re = jnp.concatenate([
                w_in[:, 0:inner], w_in[:, o_xbc:o_xbc + inner], w_in[:, o_q:o_q + dq],
                w_in[:, o_xbc + inner:o_dt], w_in[:, o_k:o_k + dkv], w_in[:, o_v:o_v + dkv]], axis=1).astype(BF16)
            z_col, xs_col, q_col, bc_col = 0, inner, 2 * inner, 2 * inner + dq
            k_col = bc_col + bcw
            v_col = k_col + dkv
            n_re = v_col + dkv
            w_dt = jnp.repeat(w_in[:, o_dt:o_dt + SSD_HEADS], HEAD_DIM, axis=1).astype(BF16)
            ones = lambda n: jnp.ones((n,), F32)
            gw = jnp.concatenate([ones(q_col), jnp.tile(ev_q_norm[j], SWA_HEADS) * (LOG2E * HEAD_DIM ** -0.5), ones(bcw),
                                  jnp.tile(ev_k_norm[j], SWA_KV_HEADS), ones(dkv)]).reshape(1, n_re)
            nm = jnp.concatenate([jnp.zeros((q_col,), F32), ones(dq), jnp.zeros((bcw,), F32), ones(dkv),
                                  jnp.zeros((dkv,), F32)]).reshape(1, n_re)
            norm_tiles = sorted({c // GROUP_TILE for c in list(range(q_col, q_col + dq, HEAD_DIM))
                                 + list(range(k_col, k_col + dkv, HEAD_DIM))})
            proj, dtx = _rms_proj(x2, ev_norm_w[j].reshape(1, d), w_re, gw, nm, gmat, norm_tiles, wd=w_dt)
            proj3 = proj.reshape(bsz, t, n_re)
            rep = lambda v: jnp.repeat(v, HEAD_DIM).reshape(1, inner)
            y_ssd = _ssd(proj3, dtx.reshape(bsz, t, inner), ev_conv_w[j], ev_conv_b[j].reshape(1, -1),
                         rep(ev_dt_bias[j]), rep(ev_a_log[j]), rep(ev_d_skip[j]),
                         ev_ssd_norm_w[j].reshape(1, inner), z_col, xs_col, bc_col)
            y_swa = _swa(proj3, ev_sinks[j], q_col, k_col, v_col)
            w_out = ev_w_out[j].astype(BF16)
            ys = [y_ssd.reshape(m, inner), y_swa.reshape(m, dq)]
            wos = [w_out[0:inner], w_out[inner:]]
        else:
            w_in = od_w_in[j].astype(BF16)
            dm = 2 * DIFF_HEADS * HEAD_DIM
            ones = jnp.ones((dm,), F32)
            gw = jnp.concatenate([jnp.tile(od_q_norm[j], 2 * DIFF_HEADS) * (LOG2E * HEAD_DIM ** -0.5),
                                  jnp.tile(od_k_norm[j], 2 * DIFF_HEADS), ones]).reshape(1, 3 * dm)
            nm = jnp.concatenate([ones, ones, jnp.zeros((dm,), F32)]).reshape(1, 3 * dm)
            norm_tiles = list(range(2 * dm // GROUP_TILE))
            (proj,) = _rms_proj(x2, od_norm_w[j].reshape(1, d), w_in, gw, nm, gmat, norm_tiles)
            lam_init = 0.8 - 0.6 * math.exp(-0.3 * layer)
            lam_p = jnp.stack([od_lam_q1[j], od_lam_k1[j], od_lam_q2[j], od_lam_k2[j]])
            y = _diff_attn(proj.reshape(bsz, t, 3 * dm), lam_p, od_sub_norm[j].reshape(1, -1), lam_init)
            ys = [y.reshape(m, dm)]
            wos = [od_w_out[j].astype(BF16)]
        x2 = _out_mlp(x2, ys, wos, mlp_norm_w[layer].reshape(1, d), mlp_w1[layer].astype(BF16),
                      mlp_w2[layer].astype(BF16))
    return x2.reshape(bsz, t, d)
```

```python
import functools
import math
import struct

import jax
import jax.numpy as jnp
from jax import lax
from jax.experimental import pallas as pl
from jax.experimental.pallas import tpu as pltpu

F32 = jnp.float32
BF16 = jnp.bfloat16

LANES = 128
V7X_VMEM_LIMIT_BYTES = 56 * 1024 * 1024

EPS = 1e-6
CHUNK = 64
HEAD_DIM = 64
GROUP_TILE = 512
SSD_HEADS = 16
SSD_STATE = 128
SSD_GROUPS = 2
SSD_INNER = SSD_HEADS * HEAD_DIM
SSD_CONV = 4
SWA_HEADS = 16
SWA_KV_HEADS = 4
SWA_BLOCK = 128
DIFF_HEADS = 8
DIFF_BLOCK = 512
DIFF_STRIP = 32
DIFF_SCORE_SLOTS = 1
DIFF_STREAM_HEADROOM = 64.0
DIFF_BLOCKS_PER_TRIP = 4
SSD_TILE = 256

NT_DIMS = (((1,), (1,)), ((), ()))
TN_DIMS = (((0,), (0,)), ((), ()))


def _alibi_slopes(n):
    return [2.0 ** (-8.0 * (i + 1) / n) for i in range(n)]


def _params(sem, vmem=V7X_VMEM_LIMIT_BYTES):
    return pltpu.CompilerParams(dimension_semantics=sem, vmem_limit_bytes=vmem)


def _rms_proj_kernel(*refs, norm_tiles, with_dt):
    if with_dt:
        x_ref, nw_ref, w_ref, gw_ref, nm_ref, g_ref, wd_ref, o_ref, dt_ref, h_ref = refs
    else:
        x_ref, nw_ref, w_ref, gw_ref, nm_ref, g_ref, o_ref, h_ref = refs
    tn = GROUP_TILE
    x = x_ref[...]
    ms = jnp.mean(x * x, axis=-1, keepdims=True)
    h_ref[...] = (x * lax.rsqrt(ms + EPS) * nw_ref[...]).astype(BF16)
    if with_dt:
        dt_ref[...] = jnp.dot(h_ref[...], wd_ref[...], preferred_element_type=F32)
    for j in range(w_ref.shape[1] // tn):
        cs = slice(j * tn, (j + 1) * tn)
        acc = jnp.dot(h_ref[...], w_ref[:, cs], preferred_element_type=F32)
        if j in norm_tiles:
            ss = jnp.dot((acc * acc).astype(BF16), g_ref[...], preferred_element_type=F32)
            inv = lax.rsqrt(ss * (1.0 / HEAD_DIM) + EPS)
            acc = acc * (jnp.where(nm_ref[:, cs] > 0.0, inv, 1.0) * gw_ref[:, cs])
        o_ref[:, cs] = acc.astype(o_ref.dtype)


def _resident(shape):
    return pl.BlockSpec(shape, lambda i: (0,) * len(shape), pipeline_mode=pl.Buffered(1))


def _rms_proj(x2d, norm_w, w, gw, nm, gmat, norm_tiles, wd=None, tm=1024):
    m, d = x2d.shape
    n = w.shape[1]
    tm = min(tm, m)
    assert m % tm == 0 and n % GROUP_TILE == 0
    with_dt = wd is not None
    in_specs = [pl.BlockSpec((tm, d), lambda i: (i, 0)), _resident((1, d)), _resident((d, n)),
                _resident((1, n)), _resident((1, n)), _resident(gmat.shape)]
    args = [x2d, norm_w, w, gw, nm, gmat]
    out_shape = [jax.ShapeDtypeStruct((m, n), BF16)]
    out_specs = [pl.BlockSpec((tm, n), lambda i: (i, 0))]
    if with_dt:
        nd = wd.shape[1]
        in_specs.append(_resident((d, nd)))
        args.append(wd)
        out_shape.append(jax.ShapeDtypeStruct((m, nd), F32))
        out_specs.append(pl.BlockSpec((tm, nd), lambda i: (i, 0)))
    res = pl.pallas_call(
        functools.partial(_rms_proj_kernel, norm_tiles=tuple(norm_tiles), with_dt=with_dt),
        grid=(m // tm,),
        in_specs=in_specs,
        out_specs=out_specs,
        out_shape=out_shape,
        scratch_shapes=[pltpu.VMEM((tm, d), BF16)],
        compiler_params=_params(("arbitrary",)),
        name="rms_proj_dt" if with_dt else "rms_proj",
    )(*args)
    return res


def _out_mlp_kernel(*refs, n_mix, tf):
    x_ref = refs[0]
    y_refs = refs[1:1 + n_mix]
    wo_refs = refs[1 + n_mix:1 + 2 * n_mix]
    nw_ref, w1_ref, w2_ref, o_ref, hn_ref = refs[1 + 2 * n_mix:]
    x1 = x_ref[...]
    for y_ref, wo_ref in zip(y_refs, wo_refs):
        x1 = x1 + jnp.dot(y_ref[...], wo_ref[...], preferred_element_type=F32)
    ms = jnp.mean(x1 * x1, axis=-1, keepdims=True)
    hn_ref[...] = (x1 * lax.rsqrt(ms + EPS) * nw_ref[...]).astype(BF16)
    o_ref[...] = x1
    for f in range(w1_ref.shape[1] // tf):
        fs = slice(f * tf, (f + 1) * tf)
        hid = jnp.dot(hn_ref[...], w1_ref[:, fs], preferred_element_type=F32)
        act = jnp.square(jnp.maximum(hid, 0.0)).astype(BF16)
        o_ref[...] += jnp.dot(act, w2_ref[fs, :], preferred_element_type=F32)


def _out_mlp(x2d, ys, wos, norm_w, w1, w2, tm=1024, tf=512):
    m, d = x2d.shape
    dff = w1.shape[1]
    tm = min(tm, m)
    assert m % tm == 0 and dff % tf == 0
    n_mix = len(ys)
    in_specs = [pl.BlockSpec((tm, d), lambda i: (i, 0))]
    in_specs += [pl.BlockSpec((tm, y.shape[1]), lambda i: (i, 0)) for y in ys]
    in_specs += [_resident(wo.shape) for wo in wos]
    in_specs += [_resident((1, d)), _resident(w1.shape), _resident(w2.shape)]
    return pl.pallas_call(
        functools.partial(_out_mlp_kernel, n_mix=n_mix, tf=tf),
        grid=(m // tm,),
        in_specs=in_specs,
        out_specs=pl.BlockSpec((tm, d), lambda i: (i, 0)),
        out_shape=jax.ShapeDtypeStruct((m, d), F32),
        scratch_shapes=[pltpu.VMEM((tm, d), BF16)],
        compiler_params=_params(("arbitrary",)),
        name="out_mlp",
    )(x2d, *ys, *wos, norm_w, w1, w2)


def _swa_kernel(sink_ref, q_ref, kp_ref, kc_ref, vp_ref, vc_ref, o_ref, bias_ref, s_ref, p_ref, rden_ref):
    n = pl.program_id(1)
    blk = SWA_BLOCK
    rows = lax.broadcasted_iota(jnp.int32, (blk, 2 * blk), 0)
    cols = lax.broadcasted_iota(jnp.int32, (blk, 2 * blk), 1)

    @pl.when(jnp.logical_and(pl.program_id(0) == 0, n == 0))
    def _():
        dist = jnp.abs(rows + blk - cols).astype(F32)
        dchunk = rows // CHUNK + 2 - cols // CHUNK
        valid = jnp.logical_and(dchunk >= 0, dchunk <= SWA_BLOCK // CHUNK)
        for h, slope in enumerate(_alibi_slopes(SWA_HEADS)):
            bias_ref[h * blk:(h + 1) * blk, :] = jnp.where(valid, (-LOG2E * slope) * dist, -jnp.inf)

    rep = SWA_HEADS // SWA_KV_HEADS
    first = jnp.where(jnp.logical_and(n == 0, cols < blk), -jnp.inf, 0.0)
    first = jnp.concatenate([first] * rep, axis=0)
    lane = lax.broadcasted_iota(jnp.int32, (1, LANES), 1)
    lo = lane < HEAD_DIM
    zero = jnp.zeros((), BF16)
    head_row = lax.broadcasted_iota(jnp.int32, (rep * blk, 1), 0) // blk

    v_both = []
    for kh in range(SWA_KV_HEADS):
        c0 = (kh // 2) * LANES
        k128 = jnp.concatenate([kp_ref[0, :, c0:c0 + LANES], kc_ref[0, :, c0:c0 + LANES]], axis=0)
        v128 = jnp.concatenate([vp_ref[0, :, c0:c0 + LANES], vc_ref[0, :, c0:c0 + LANES]], axis=0)
        k_sw = pltpu.roll(k128, HEAD_DIM, 1)
        v_sw = pltpu.roll(v128, HEAD_DIM, 1)
        k_both = jnp.where(lo, k128, k_sw) if kh % 2 == 0 else jnp.where(lo, k_sw, k128)
        v_both.append(jnp.where(lo, v128, v_sw) if kh % 2 == 0 else jnp.where(lo, v_sw, v128))
        parts = []
        for pr in range(rep // 2):
            pair = kh * (rep // 2) + pr
            q128 = q_ref[0, :, pair * LANES:(pair + 1) * LANES]
            parts += [jnp.where(lo, q128, zero), jnp.where(lo, zero, q128)]
        q4 = jnp.concatenate(parts, axis=0)
        s = lax.dot_general(q4, k_both, NT_DIMS, preferred_element_type=F32)
        s_ref[kh] = s + bias_ref[kh * rep * blk:(kh + 1) * rep * blk, :] + first
    for kh in range(SWA_KV_HEADS):
        sink = jnp.zeros((rep * blk, 1), F32)
        for r in range(rep):
            sink = jnp.where(head_row == r, sink_ref[kh * rep + r], sink)
        sink = sink * LOG2E
        s = s_ref[kh]
        m = jnp.maximum(jnp.max(s, axis=-1, keepdims=True), sink)
        ex = jnp.exp2(s - m)
        den = jnp.sum(ex, axis=-1, keepdims=True) + jnp.exp2(sink - m)
        p_ref[kh] = ex.astype(BF16)
        rden_ref[kh] = jnp.broadcast_to(1.0 / den, (rep * blk, LANES))
    for kh in range(SWA_KV_HEADS):
        o4 = jnp.dot(p_ref[kh], v_both[kh], preferred_element_type=F32) * rden_ref[kh]
        for pr in range(rep // 2):
            pair = kh * (rep // 2) + pr
            out = jnp.where(lo, o4[2 * pr * blk:(2 * pr + 1) * blk], o4[(2 * pr + 1) * blk:(2 * pr + 2) * blk])
            o_ref[0, :, pair * LANES:(pair + 1) * LANES] = out.astype(o_ref.dtype)


def _swa(proj3, sinks, q_col, k_col, v_col):
    b, t, _ = proj3.shape
    blk = SWA_BLOCK
    nb = t // blk
    dq = SWA_HEADS * HEAD_DIM
    dkv = SWA_KV_HEADS * HEAD_DIM
    qb, kb, vb = q_col // dq, k_col // dkv, v_col // dkv
    prev = lambda n: jnp.maximum(n - 1, 0)
    return pl.pallas_call(
        _swa_kernel,
        grid=(b, nb),
        in_specs=[
            pl.BlockSpec(memory_space=pltpu.SMEM),
            pl.BlockSpec((1, blk, dq), lambda i, n: (i, n, qb)),
            pl.BlockSpec((1, blk, dkv), lambda i, n: (i, prev(n), kb)),
            pl.BlockSpec((1, blk, dkv), lambda i, n: (i, n, kb)),
            pl.BlockSpec((1, blk, dkv), lambda i, n: (i, prev(n), vb)),
            pl.BlockSpec((1, blk, dkv), lambda i, n: (i, n, vb)),
        ],
        out_specs=pl.BlockSpec((1, blk, dq), lambda i, n: (i, n, 0)),
        out_shape=jax.ShapeDtypeStruct((b, t, dq), BF16),
        scratch_shapes=[
            pltpu.VMEM((SWA_HEADS * blk, 2 * blk), F32),
            pltpu.VMEM((SWA_KV_HEADS, SWA_HEADS // SWA_KV_HEADS * blk, 2 * blk), F32),
            pltpu.VMEM((SWA_KV_HEADS, SWA_HEADS // SWA_KV_HEADS * blk, 2 * blk), BF16),
            pltpu.VMEM((SWA_KV_HEADS, SWA_HEADS // SWA_KV_HEADS * blk, LANES), F32),
        ],
        compiler_params=_params(("arbitrary", "arbitrary")),
        name="swa",
    )(sinks, proj3, proj3, proj3, proj3, proj3)


def _split_bf16(a):
    hi = a.astype(BF16)
    lo = (a - hi.astype(F32)).astype(BF16)
    return hi, lo


def _ssd_kernel(z_ref, xs_ref, bc_ref, xsp_ref, bcp_ref, dt_ref, cw_ref, cb_ref, dtb_ref, alog_ref, dskip_ref,
                nw_ref, o_ref, st_ref):
    tt = SSD_TILE
    inner = SSD_INNER
    gw = inner // SSD_GROUPS
    ns = SSD_STATE
    t = pl.program_id(1)

    half = tt // 2

    @pl.when(t == 0)
    def _():
        st_ref[...] = jnp.zeros_like(st_ref)

    si = lax.broadcasted_iota(jnp.int32, ((SSD_CONV - 1) * half, tt), 0)
    sc = lax.broadcasted_iota(jnp.int32, ((SSD_CONV - 1) * half, tt), 1)
    sel = (sc == (si % half) + half - (SSD_CONV - 1) + si // half).astype(BF16)

    def conv_silu(cur_ref, prev_ref, c0, c1):
        outs = []
        for hh in range(2):
            cur = cur_ref[0, hh * half:(hh + 1) * half, :]
            if hh == 0:
                before = jnp.where(t > 0, prev_ref[0], jnp.zeros((), BF16))
            else:
                before = cur_ref[0, 0:half, :]
            sh = jnp.dot(sel, jnp.concatenate([before, cur], axis=0), preferred_element_type=F32)
            conv = cb_ref[:, c0:c1] + cw_ref[SSD_CONV - 1:SSD_CONV, c0:c1] * cur.astype(F32)
            for k in range(SSD_CONV - 1):
                conv = conv + cw_ref[k:k + 1, c0:c1] * sh[k * half:(k + 1) * half]
            outs.append(conv * jax.nn.sigmoid(conv))
        return jnp.concatenate(outs, axis=0)

    xs = conv_silu(xs_ref, xsp_ref, 0, inner)
    bcm = conv_silu(bc_ref, bcp_ref, inner, inner + 2 * SSD_GROUPS * ns).astype(BF16)
    bm = bcm[:, 0:SSD_GROUPS * ns]
    cm = bcm[:, SSD_GROUPS * ns:]

    dtx = dt_ref[0] + dtb_ref[...]
    dt_c = jnp.maximum(dtx, 0.0) + jnp.log(1.0 + jnp.exp(-jnp.abs(dtx)))
    adt_c = dt_c * (-jnp.exp(alog_ref[...]))
    er = lax.broadcasted_iota(jnp.int32, (LANES, inner), 0)
    ec = lax.broadcasted_iota(jnp.int32, (LANES, inner), 1)
    expand = (er == ec // HEAD_DIM).astype(BF16)
    d_hi, d_lo = _split_bf16(dt_c)
    dt = jnp.dot(d_hi, expand, preferred_element_type=F32) + jnp.dot(d_lo, expand, preferred_element_type=F32)
    a_hi, a_lo = _split_bf16(adt_c)
    a_hi = jnp.dot(a_hi, expand, preferred_element_type=F32).astype(BF16)
    a_lo = jnp.dot(a_lo, expand, preferred_element_type=F32).astype(BF16)

    ri = lax.broadcasted_iota(jnp.int32, (tt, tt), 0)
    ci = lax.broadcasted_iota(jnp.int32, (tt, tt), 1)
    tril = jnp.logical_and(ri // CHUNK == ci // CHUNK, ci <= ri).astype(BF16)
    rl = lax.broadcasted_iota(jnp.int32, (tt, 1), 0) % CHUNK
    cl = lax.broadcasted_iota(jnp.int32, (1, inner), 1) % HEAD_DIM

    a_cum = jnp.dot(tril, a_hi, preferred_element_type=F32) + jnp.dot(tril, a_lo, preferred_element_type=F32)
    later = rl > cl
    zero_b = jnp.zeros((), BF16)
    seg = (jnp.dot(tril, jnp.where(later, a_hi, zero_b), preferred_element_type=F32)
           + jnp.dot(tril, jnp.where(later, a_lo, zero_b), preferred_element_type=F32))
    decay = jnp.exp(jnp.where(cl <= rl, seg, -jnp.inf))

    xdt = xs * dt
    quad = lax.broadcasted_iota(jnp.int32, (CHUNK, 4 * HEAD_DIM), 1) // HEAD_DIM
    zero = jnp.zeros((), BF16)

    for c in range(tt // CHUNK):
        r0 = c * CHUNK
        rs = slice(r0, r0 + CHUNK)
        acum_c = a_cum[rs]
        a_last = a_cum[r0 + CHUNK - 1:r0 + CHUNK]
        xdt_c = xdt[rs]
        xdt_b = xdt_c.astype(BF16)
        cbs = []
        for g in range(SSD_GROUPS):
            b_g = bm[rs, g * ns:(g + 1) * ns]
            c_g = cm[rs, g * ns:(g + 1) * ns]
            b_tiled = jnp.concatenate([b_g] * (gw // HEAD_DIM), axis=0)
            cbs.append(lax.dot_general(c_g, b_tiled, NT_DIMS, preferred_element_type=F32))
        gmat = (jnp.concatenate(cbs, axis=1) * decay[rs]).astype(BF16)
        yd = []
        for qd in range(inner // (4 * HEAD_DIM)):
            cs = slice(qd * 4 * HEAD_DIM, (qd + 1) * 4 * HEAD_DIM)
            xq = xdt_b[:, cs]
            xbd = jnp.concatenate([jnp.where(quad == jj, xq, zero) for jj in range(4)], axis=0)
            yd.append(jnp.dot(gmat[:, cs], xbd, preferred_element_type=F32))
        y = jnp.concatenate(yd, axis=1)
        din = jnp.exp(acum_c)
        x_end = (xdt_c * jnp.exp(a_last - acum_c)).astype(BF16)
        cdec = jnp.exp(a_last)
        yo = []
        for g in range(SSD_GROUPS):
            gs = slice(g * gw, (g + 1) * gw)
            b_g = bm[rs, g * ns:(g + 1) * ns]
            c_g = cm[rs, g * ns:(g + 1) * ns]
            st_g = st_ref[:, gs]
            yo.append(jnp.dot(c_g, st_g.astype(BF16), preferred_element_type=F32))
            upd = lax.dot_general(b_g, x_end[:, gs], TN_DIMS, preferred_element_type=F32)
            st_ref[:, gs] = st_g * cdec[:, gs] + upd
        y = y + jnp.concatenate(yo, axis=1) * din
        y = y + xs[rs] * dskip_ref[...]
        zz = z_ref[0, rs, :].astype(F32)
        y = y * (zz * jax.nn.sigmoid(zz))
        outs = []
        for g in range(SSD_GROUPS):
            y_g = y[:, g * gw:(g + 1) * gw]
            ms = jnp.mean(y_g * y_g, axis=-1, keepdims=True)
            outs.append(y_g * lax.rsqrt(ms + EPS))
        o_ref[0, rs, :] = (jnp.concatenate(outs, axis=1) * nw_ref[...]).astype(o_ref.dtype)


def _ssd(proj3, dtx3, conv_w, conv_b, dtb, alog, dskip, nw, z_col, xs_col, bc_col):
    b, t, _ = proj3.shape
    tt = SSD_TILE
    assert t % tt == 0
    inner = SSD_INNER
    bcw = 2 * SSD_GROUPS * SSD_STATE
    cdim = inner + bcw
    full = lambda shape: pl.BlockSpec(shape, lambda i, s: (0,) * len(shape))
    return pl.pallas_call(
        _ssd_kernel,
        grid=(b, t // tt),
        in_specs=[
            pl.BlockSpec((1, tt, inner), lambda i, s: (i, s, z_col // inner)),
            pl.BlockSpec((1, tt, inner), lambda i, s: (i, s, xs_col // inner)),
            pl.BlockSpec((1, tt, bcw), lambda i, s: (i, s, bc_col // bcw)),
            pl.BlockSpec((1, tt // 2, inner), lambda i, s: (i, jnp.maximum(2 * s - 1, 0), xs_col // inner)),
            pl.BlockSpec((1, tt // 2, bcw), lambda i, s: (i, jnp.maximum(2 * s - 1, 0), bc_col // bcw)),
            pl.BlockSpec((1, tt, LANES), lambda i, s: (i, s, 0)),
            full((SSD_CONV, cdim)), full((1, cdim)), full((1, LANES)), full((1, LANES)),
            full((1, inner)), full((1, inner)),
        ],
        out_specs=pl.BlockSpec((1, tt, inner), lambda i, s: (i, s, 0)),
        out_shape=jax.ShapeDtypeStruct((b, t, inner), BF16),
        scratch_shapes=[pltpu.VMEM((SSD_STATE, inner), F32)],
        compiler_params=_params(("arbitrary", "arbitrary")),
        name="ssd",
    )(proj3, proj3, proj3, proj3, proj3, dtx3, conv_w, conv_b, dtb, alog, dskip, nw)


def _round_to_bf16(x):
    bits = struct.unpack("<I", struct.pack("<f", x))[0]
    bits = (bits + 0x7FFF + ((bits >> 16) & 1)) & 0xFFFF0000
    return struct.unpack("<f", struct.pack("<I", bits))[0]


def _bf16_terms(x, n):
    out = []
    for _ in range(n):
        out.append(_round_to_bf16(x))
        x -= out[-1]
    return out


LOG2E = math.log2(math.e)
LOG2E_TERMS = _bf16_terms(LOG2E, 3)
DIFF_BIAS_SPLIT = 16


def _diff_attn_kernel(slope_ref, lam_ref, q_ref, k_ref, v_ref, sn_ref, o_ref, tdiag_ref, kaug_ref, vt_ref,
                      *scratch, lam_init, nq):
    blk = DIFF_BLOCK
    strip = DIFF_STRIP
    ns = DIFF_SCORE_SLOTS
    qm_ref, p_ref, m_ref, l_ref, acc_ref, lfin_ref, accfin_ref = (scratch[2 * i:2 * i + 2] for i in range(7))
    s_ref = [scratch[14 + 2 * i:16 + 2 * i] for i in range(ns)]
    slope = jnp.full((1, 1), slope_ref[pl.program_id(1)], F32)
    kk =lax.broadcasted_iota(jnp.int32, (blk, blk), 0)
    qq = lax.broadcasted_iota(jnp.int32, (blk, blk), 1)
    tdiag_ref[...] = jnp.where(kk // CHUNK > qq // CHUNK, -jnp.inf,
                               jnp.where(kk > qq, (2.0 * LOG2E) * slope * (qq - kk).astype(F32), 0.0))

    lam_p = lam_ref[...]
    lam = (jnp.exp(jnp.sum(lam_p[0:1] * lam_p[1:2], axis=-1, keepdims=True))
           - jnp.exp(jnp.sum(lam_p[2:3] * lam_p[3:4], axis=-1, keepdims=True)) + lam_init)

    lane = lax.broadcasted_iota(jnp.int32, (blk, LANES), 1)
    pos = lax.broadcasted_iota(jnp.int32, (blk, LANES), 0)
    lo = lane < HEAD_DIM
    nt = len(LOG2E_TERMS)
    b_hi = slope * (DIFF_BIAS_SPLIT * (pos // DIFF_BIAS_SPLIT)).astype(F32)
    b_lo = slope * (pos % DIFF_BIAS_SPLIT).astype(F32)
    kbias, qconst = [], []
    for base in (HEAD_DIM, 0):
        rel = lane - base
        kbias.append(jnp.where((rel >= 0) & (rel < nt), b_hi,
                               jnp.where((rel >= nt) & (rel < 2 * nt), b_lo, 0.0)).astype(BF16))
        qc = jnp.zeros((blk, LANES), F32)
        for i, term in enumerate(LOG2E_TERMS):
            qc = jnp.where((rel == i) | (rel == nt + i), term, qc)
        qconst.append(qc.astype(BF16))
    for j in range(nq):
        k = k_ref[0, j * blk:(j + 1) * blk, :]
        kaug_ref[0, j] = jnp.where(lo, k, kbias[0])
        kaug_ref[1, j] = jnp.where(lo, kbias[1], k)
        vt_ref[j] = v_ref[0, j * blk:(j + 1) * blk, :].astype(F32).T.astype(BF16)

    def scores(slot, br, j, diag):
        s = lax.dot_general(kaug_ref[br, j], qm_ref[br][...], NT_DIMS, preferred_element_type=F32)
        if diag:
            s = s + tdiag_ref[...]
        s_ref[slot][br][...] = s
        return jnp.max(s, axis=0, keepdims=True)

    def exp_strips(br, src, off):
        lp = [None, None]
        for i, r in enumerate(range(0, blk, strip)):
            p = jnp.exp2(src[r:r + strip, :] - off)
            part = p[0:8]
            for g in range(8, strip, 8):
                part = part + p[g:g + 8]
            lp[i % 2] = part if lp[i % 2] is None else lp[i % 2] + part
            p_ref[br][r:r + strip, :] = p.astype(BF16)
        return lp[0] + lp[1]

    def absorb(slot, br, j, mx, shift):
        m_old = m_ref[br][0:1, :]
        m_new = jnp.maximum(m_old, mx + shift)
        alpha = jnp.exp2(m_old - m_new)
        m_ref[br][...] = jnp.broadcast_to(m_new, m_ref[br].shape)
        lp = exp_strips(br, s_ref[slot][br], m_new - shift)
        l_ref[br][...] = alpha * l_ref[br][...] + lp
        pv = jnp.dot(vt_ref[j], p_ref[br][...], preferred_element_type=F32)
        acc_ref[br][...] = alpha * acc_ref[br][...] + pv

    def raw_scores(br, j):
        return lax.dot_general(kaug_ref[br, j], qm_ref[br][...], NT_DIMS, preferred_element_type=F32)

    def absorb_streaming(br, j, s, shift):
        m_old = m_ref[br][0:1, :]
        excess = jnp.max(s, axis=0, keepdims=True) + shift - m_old
        lp = exp_strips(br, s, m_old - shift)
        pv = jnp.dot(vt_ref[j], p_ref[br][...], preferred_element_type=F32)
        up = jnp.maximum(excess, 0.0)
        alpha = jnp.exp2(-up)
        m_ref[br][...] = jnp.broadcast_to(m_old + up, m_ref[br].shape)
        l_ref[br][...] = alpha * (l_ref[br][...] + lp)
        acc_ref[br][...] = alpha * (acc_ref[br][...] + pv)
        return excess

    zero_shift = jnp.zeros((1, 1), F32)

    def shift_of(qi, j):
        return (-LOG2E * blk) * slope * jnp.full((1, 1), qi - j, jnp.int32).astype(F32)

    def reset(br, m_init):
        m_ref[br][...] = jnp.broadcast_to(m_init, m_ref[br].shape)
        l_ref[br][...] = jnp.zeros(l_ref[br].shape, F32)
        acc_ref[br][...] = jnp.zeros(acc_ref[br].shape, F32)

    def load_queries(qi):
        q = q_ref[0, pl.ds(pl.multiple_of(qi * blk, blk), blk), :]
        qm_ref[0][...] = jnp.where(lo, q, qconst[0])
        qm_ref[1][...] = jnp.where(lo, qconst[1], q)

    def write_output(qi, l_src, acc_src):
        r1 = 1.0 / jnp.sum(l_src[0][...], axis=0, keepdims=True)
        r2 = lam / jnp.sum(l_src[1][...], axis=0, keepdims=True)
        o = (acc_src[0][...] * r1 - acc_src[1][...] * r2).T
        ms = jnp.mean(o * o, axis=-1, keepdims=True)
        o = o * lax.rsqrt(ms + EPS) * sn_ref[...] * (1.0 - lam_init)
        q0 = qi * blk if isinstance(qi, int) else pl.multiple_of(qi * blk, blk)
        o_ref[0, pl.ds(q0, blk), :] = o.astype(o_ref.dtype)

    for br in range(2):
        lfin_ref[br][...] = jnp.ones(lfin_ref[br].shape, F32)
        accfin_ref[br][...] = jnp.zeros(accfin_ref[br].shape, F32)

    def q_step(qi, worst):
        write_output(jnp.maximum(qi - 1, 0), lfin_ref, accfin_ref)
        load_queries(qi)
        s = []
        for br in range(2):
            prod = (kaug_ref[br, qi].astype(F32) * qm_ref[br][...].astype(F32)).astype(BF16)
            own = lax.dot_general(jnp.ones((8, LANES), BF16), prod, NT_DIMS, preferred_element_type=F32)
            reset(br, own[0:1, :])
            s.append(raw_scores(br, qi) + tdiag_ref[...])
        worst = jnp.maximum(worst, jnp.maximum(absorb_streaming(0, qi, s[0], zero_shift),
                                               absorb_streaming(1, qi, s[1], zero_shift)))

        def stream_step(j, worst):
            shift = shift_of(qi, j)
            s = [raw_scores(br, j) for br in range(2)]
            e0 = absorb_streaming(0, j, s[0], shift)
            e1 = absorb_streaming(1, j, s[1], shift)
            return jnp.maximum(worst, jnp.maximum(e0, e1))

        nu = DIFF_BLOCKS_PER_TRIP

        def run_blocks(first, count, w):
            for i in range(count):
                w = stream_step(first + i, w)
            return w

        worst = lax.fori_loop(0, qi // nu, lambda t, w: run_blocks(nu * t, nu, w), worst)
        rem = qi % nu
        worst = lax.switch(rem, [functools.partial(run_blocks, qi - rem, r) for r in range(nu)], worst)
        for br in range(2):
            lfin_ref[br][...] = l_ref[br][...]
            accfin_ref[br][...] = acc_ref[br][...]
        return worst

    worst = lax.fori_loop(0, nq, q_step, jnp.full((1, blk), -jnp.inf, F32))
    write_output(nq - 1, lfin_ref, accfin_ref)

    @pl.when(jnp.max(worst) > DIFF_STREAM_HEADROOM)
    def _():
        def q_step_two_pass(qi, _):
            load_queries(qi)
            mx = [scores(0, br, qi, True) for br in range(2)]
            for br in range(2):
                reset(br, -jnp.inf)
                absorb(0, br, qi, mx[br], zero_shift)

            def kv_step(j, _):
                mx = [scores(0, br, j, False) for br in range(2)]
                for br in range(2):
                    absorb(0, br, j, mx[br], shift_of(qi, j))
                return 0

            lax.fori_loop(0, qi, kv_step, 0)
            write_output(qi, l_ref, acc_ref)
            return 0

        lax.fori_loop(0, nq, q_step_two_pass, 0)


def _diff_attn(proj3, lam_p, sub_norm, lam_init):
    b, t, _ = proj3.shape
    blk = DIFF_BLOCK
    assert t % blk == 0
    hd = 2 * HEAD_DIM
    nh = DIFF_HEADS
    assert all(math.log2(s).is_integer() for s in _alibi_slopes(nh))
    return pl.pallas_call(
        functools.partial(_diff_attn_kernel, lam_init=lam_init, nq=t // blk),
        grid=(b, nh),
        in_specs=[
            pl.BlockSpec(memory_space=pltpu.SMEM),
            pl.BlockSpec((4, HEAD_DIM), lambda i, h: (0, 0)),
            pl.BlockSpec((1, t, hd), lambda i, h: (i, 0, h)),
            pl.BlockSpec((1, t, hd), lambda i, h: (i, 0, nh + h)),
            pl.BlockSpec((1, t, hd), lambda i, h: (i, 0, 2 * nh + h)),
            pl.BlockSpec((1, hd), lambda i, h: (0, 0)),
        ],
        out_specs=pl.BlockSpec((1, t, hd), lambda i, h: (i, 0, h)),
        out_shape=jax.ShapeDtypeStruct((b, t, nh * hd), BF16),
        scratch_shapes=[
            pltpu.VMEM((blk, blk), F32), pltpu.VMEM((2, t // blk, blk, hd), BF16),
            pltpu.VMEM((t // blk, hd, blk), BF16),
            *[pltpu.VMEM((blk, hd), BF16)] * 2, *[pltpu.VMEM((blk, blk), BF16)] * 2,
            *[pltpu.VMEM((8, blk), F32)] * 2, *[pltpu.VMEM((8, blk), F32)] * 2, *[pltpu.VMEM((hd, blk), F32)] * 2,
            *[pltpu.VMEM((8, blk), F32)] * 2, *[pltpu.VMEM((hd, blk), F32)] * 2,
            *[pltpu.VMEM((blk, blk), F32)] * (2 * DIFF_SCORE_SLOTS),
        ],
        compiler_params=_params(("arbitrary", "arbitrary")),
        name="diff_attn",
    )(jnp.asarray(_alibi_slopes(nh), F32), lam_p, proj3, proj3, proj3, sub_norm)


def _group_sum_matrix():
    idx = jnp.arange(GROUP_TILE) // HEAD_DIM
    return (idx[:, None] == idx[None, :]).astype(BF16)


def kernel(x, ev_norm_w, ev_w_in, ev_conv_w, ev_conv_b, ev_dt_bias, ev_a_log, ev_d_skip, ev_ssd_norm_w,
           ev_q_norm, ev_k_norm, ev_sinks, ev_w_out,
           od_norm_w, od_w_in, od_q_norm, od_k_norm, od_lam_q1, od_lam_k1, od_lam_q2, od_lam_k2,
           od_sub_norm, od_w_out, mlp_norm_w, mlp_w1, mlp_w2):
    bsz, t, d = x.shape
    m = bsz * t
    depth = mlp_w1.shape[0]
    gmat = _group_sum_matrix()
    x2 = x.reshape(m, d)
    for layer in range(depth):
        j = layer // 2
        if layer % 2 == 0:
            w_in = ev_w_in[j]
            inner = SSD_INNER
            bcw = 2 * SSD_GROUPS * SSD_STATE
            dq = SWA_HEADS * HEAD_DIM
            dkv = SWA_KV_HEADS * HEAD_DIM
            o_xbc = inner
            o_dt = o_xbc + inner + bcw
            o_q = o_dt + SSD_HEADS
            o_k = o_q + dq
            o_v = o_k + dkv
            w_re = jnp.concatenate([
                w_in[:, 0:inner], w_in[:, o_xbc:o_xbc + inner], w_in[:, o_q:o_q + dq],
                w_in[:, o_xbc + inner:o_dt], w_in[:, o_k:o_k + dkv], w_in[:, o_v:o_v + dkv]], axis=1).astype(BF16)
            z_col, xs_col, q_col, bc_col = 0, inner, 2 * inner, 2 * inner + dq
            k_col = bc_col + bcw
            v_col = k_col + dkv
            n_re = v_col + dkv
            pad_heads = lambda a: jnp.pad(a, [(0, 0)] * (a.ndim - 1) + [(0, LANES - SSD_HEADS)])
            w_dt = pad_heads(w_in[:, o_dt:o_dt + SSD_HEADS]).astype(BF16)
            ones = lambda n: jnp.ones((n,), F32)
            gw = jnp.concatenate([ones(q_col), jnp.tile(ev_q_norm[j], SWA_HEADS) * (LOG2E * HEAD_DIM ** -0.5), ones(bcw),
                                  jnp.tile(ev_k_norm[j], SWA_KV_HEADS), ones(dkv)]).reshape(1, n_re)
            nm = jnp.concatenate([jnp.zeros((q_col,), F32), ones(dq), jnp.zeros((bcw,), F32), ones(dkv),
                                  jnp.zeros((dkv,), F32)]).reshape(1, n_re)
            norm_tiles = sorted({c // GROUP_TILE for c in list(range(q_col, q_col + dq, HEAD_DIM))
                                 + list(range(k_col, k_col + dkv, HEAD_DIM))})
            proj, dtx = _rms_proj(x2, ev_norm_w[j].reshape(1, d), w_re, gw, nm, gmat, norm_tiles, wd=w_dt)
            proj3 = proj.reshape(bsz, t, n_re)
            rep = lambda v: jnp.repeat(v, HEAD_DIM).reshape(1, inner)
            y_ssd = _ssd(proj3, dtx.reshape(bsz, t, LANES), ev_conv_w[j], ev_conv_b[j].reshape(1, -1),
                         pad_heads(ev_dt_bias[j]).reshape(1, LANES), pad_heads(ev_a_log[j]).reshape(1, LANES),
                         rep(ev_d_skip[j]),
                         ev_ssd_norm_w[j].reshape(1, inner), z_col, xs_col, bc_col)
            y_swa = _swa(proj3, ev_sinks[j], q_col, k_col, v_col)
            w_out = ev_w_out[j].astype(BF16)
            ys = [y_ssd.reshape(m, inner), y_swa.reshape(m, dq)]
            wos = [w_out[0:inner], w_out[inner:]]
        else:
            w_in = od_w_in[j].astype(BF16)
            dm = 2 * DIFF_HEADS * HEAD_DIM
            ones = jnp.ones((dm,), F32)
            gw = jnp.concatenate([jnp.tile(od_q_norm[j], 2 * DIFF_HEADS) * (LOG2E * HEAD_DIM ** -0.5),
                                  jnp.tile(od_k_norm[j], 2 * DIFF_HEADS), ones]).reshape(1, 3 * dm)
            nm = jnp.concatenate([ones, ones, jnp.zeros((dm,), F32)]).reshape(1, 3 * dm)
            norm_tiles = list(range(2 * dm // GROUP_TILE))
            (proj,) = _rms_proj(x2, od_norm_w[j].reshape(1, d), w_in, gw, nm, gmat, norm_tiles)
            lam_init = 0.8 - 0.6 * math.exp(-0.3 * layer)
            lam_p = jnp.stack([od_lam_q1[j], od_lam_k1[j], od_lam_q2[j], od_lam_k2[j]])
            y = _diff_attn(proj.reshape(bsz, t, 3 * dm), lam_p, od_sub_norm[j].reshape(1, -1), lam_init)
            ys = [y.reshape(m, dm)]
            wos = [od_w_out[j].astype(BF16)]
        x2 = _out_mlp(x2, ys, wos, mlp_norm_w[layer].reshape(1, d), mlp_w1[layer].astype(BF16),
                      mlp_w2[layer].astype(BF16))
    return x2.reshape(bsz, t, d)
```

```python
import functools
import math
import struct

import jax
import jax.numpy as jnp
from jax import lax
from jax.experimental import pallas as pl
from jax.experimental.pallas import tpu as pltpu

F32 = jnp.float32
BF16 = jnp.bfloat16

LANES = 128
V7X_VMEM_LIMIT_BYTES = 56 * 1024 * 1024

EPS = 1e-6
CHUNK = 64
HEAD_DIM = 64
GROUP_TILE = 512
SSD_HEADS = 16
SSD_STATE = 128
SSD_GROUPS = 2
SSD_INNER = SSD_HEADS * HEAD_DIM
SSD_CONV = 4
SWA_HEADS = 16
SWA_KV_HEADS = 4
SWA_BLOCK = 128
DIFF_HEADS = 8
DIFF_BLOCK = 512
DIFF_STRIP = 32
DIFF_SCORE_SLOTS = 1
DIFF_STREAM_HEADROOM = 64.0
DIFF_BLOCKS_PER_TRIP = 4
SSD_TILE = 256

NT_DIMS = (((1,), (1,)), ((), ()))
TN_DIMS = (((0,), (0,)), ((), ()))


def _alibi_slopes(n):
    return [2.0 ** (-8.0 * (i + 1) / n) for i in range(n)]


def _params(sem, vmem=V7X_VMEM_LIMIT_BYTES):
    return pltpu.CompilerParams(dimension_semantics=sem, vmem_limit_bytes=vmem)


def _rms_proj_kernel(*refs, norm_tiles, extra):
    if extra is None:
        x_ref, nw_ref, w_ref, gw_ref, nm_ref, g_ref, o_ref, h_ref = refs
    else:
        x_ref, nw_ref, w_ref, gw_ref, nm_ref, g_ref, w2_ref, o_ref, o2_ref, h_ref = refs
    tn = GROUP_TILE
    x = x_ref[...]
    ms = jnp.mean(x * x, axis=-1, keepdims=True)
    h_ref[...] = (x * lax.rsqrt(ms + EPS) * nw_ref[...]).astype(BF16)
    if extra == "dt":
        o2_ref[...] = jnp.dot(h_ref[...], w2_ref[...], preferred_element_type=F32)
    elif extra == "transposed":
        for p in range(o2_ref.shape[0]):
            rows = h_ref[p * DIFF_BLOCK:(p + 1) * DIFF_BLOCK, :]
            o2_ref[p] = lax.dot_general(w2_ref[...], rows, NT_DIMS, preferred_element_type=F32).astype(o2_ref.dtype)
    for j in range(w_ref.shape[1] // tn):
        cs = slice(j * tn, (j + 1) * tn)
        acc = jnp.dot(h_ref[...], w_ref[:, cs], preferred_element_type=F32)
        if j in norm_tiles:
            ss = jnp.dot((acc * acc).astype(BF16), g_ref[...], preferred_element_type=F32)
            inv = lax.rsqrt(ss * (1.0 / HEAD_DIM) + EPS)
            acc = acc * (jnp.where(nm_ref[:, cs] > 0.0, inv, 1.0) * gw_ref[:, cs])
        o_ref[:, cs] = acc.astype(o_ref.dtype)


def _resident(shape):
    return pl.BlockSpec(shape, lambda i: (0,) * len(shape), pipeline_mode=pl.Buffered(1))


def _rms_proj(x2d, norm_w, w, gw, nm, gmat, norm_tiles, wd=None, wt=None, tm=1024):
    m, d = x2d.shape
    n = w.shape[1]
    tm = min(tm, m)
    assert m % tm == 0 and n % GROUP_TILE == 0 and (wd is None or wt is None)
    extra = "dt" if wd is not None else "transposed" if wt is not None else None
    in_specs = [pl.BlockSpec((tm, d), lambda i: (i, 0)), _resident((1, d)), _resident((d, n)),
                _resident((1, n)), _resident((1, n)), _resident(gmat.shape)]
    args = [x2d, norm_w, w, gw, nm, gmat]
    out_shape = [jax.ShapeDtypeStruct((m, n), BF16)]
    out_specs = [pl.BlockSpec((tm, n), lambda i: (i, 0))]
    if extra == "dt":
        nd = wd.shape[1]
        in_specs.append(_resident((d, nd)))
        args.append(wd)
        out_shape.append(jax.ShapeDtypeStruct((m, nd), F32))
        out_specs.append(pl.BlockSpec((tm, nd), lambda i: (i, 0)))
    elif extra == "transposed":
        assert tm % DIFF_BLOCK == 0
        nt = wt.shape[0]
        in_specs.append(_resident((nt, d)))
        args.append(wt)
        out_shape.append(jax.ShapeDtypeStruct((m // DIFF_BLOCK, nt, DIFF_BLOCK), BF16))
        out_specs.append(pl.BlockSpec((tm // DIFF_BLOCK, nt, DIFF_BLOCK), lambda i: (i, 0, 0)))
    res = pl.pallas_call(
        functools.partial(_rms_proj_kernel, norm_tiles=tuple(norm_tiles), extra=extra),
        grid=(m // tm,),
        in_specs=in_specs,
        out_specs=out_specs,
        out_shape=out_shape,
        scratch_shapes=[pltpu.VMEM((tm, d), BF16)],
        compiler_params=_params(("arbitrary",)),
        name={None: "rms_proj", "dt": "rms_proj_dt", "transposed": "rms_proj_t"}[extra],
    )(*args)
    return res


def _out_mlp_kernel(*refs, n_mix, tf):
    x_ref = refs[0]
    y_refs = refs[1:1 + n_mix]
    wo_refs = refs[1 + n_mix:1 + 2 * n_mix]
    nw_ref, w1_ref, w2_ref, o_ref, hn_ref = refs[1 + 2 * n_mix:]
    x1 = x_ref[...]
    for y_ref, wo_ref in zip(y_refs, wo_refs):
        x1 = x1 + jnp.dot(y_ref[...], wo_ref[...], preferred_element_type=F32)
    ms = jnp.mean(x1 * x1, axis=-1, keepdims=True)
    hn_ref[...] = (x1 * lax.rsqrt(ms + EPS) * nw_ref[...]).astype(BF16)
    o_ref[...] = x1
    for f in range(w1_ref.shape[1] // tf):
        fs = slice(f * tf, (f + 1) * tf)
        hid = jnp.dot(hn_ref[...], w1_ref[:, fs], preferred_element_type=F32)
        act = jnp.square(jnp.maximum(hid, 0.0)).astype(BF16)
        o_ref[...] += jnp.dot(act, w2_ref[fs, :], preferred_element_type=F32)


def _out_mlp(x2d, ys, wos, norm_w, w1, w2, tm=1024, tf=512):
    m, d = x2d.shape
    dff = w1.shape[1]
    tm = min(tm, m)
    assert m % tm == 0 and dff % tf == 0
    n_mix = len(ys)
    in_specs = [pl.BlockSpec((tm, d), lambda i: (i, 0))]
    in_specs += [pl.BlockSpec((tm, y.shape[1]), lambda i: (i, 0)) for y in ys]
    in_specs += [_resident(wo.shape) for wo in wos]
    in_specs += [_resident((1, d)), _resident(w1.shape), _resident(w2.shape)]
    return pl.pallas_call(
        functools.partial(_out_mlp_kernel, n_mix=n_mix, tf=tf),
        grid=(m // tm,),
        in_specs=in_specs,
        out_specs=pl.BlockSpec((tm, d), lambda i: (i, 0)),
        out_shape=jax.ShapeDtypeStruct((m, d), F32),
        scratch_shapes=[pltpu.VMEM((tm, d), BF16)],
        compiler_params=_params(("arbitrary",)),
        name="out_mlp",
    )(x2d, *ys, *wos, norm_w, w1, w2)


def _swa_kernel(sink_ref, q_ref, kp_ref, kc_ref, vp_ref, vc_ref, o_ref, bias_ref, s_ref, p_ref, rden_ref):
    n = pl.program_id(1)
    blk = SWA_BLOCK
    rows = lax.broadcasted_iota(jnp.int32, (blk, 2 * blk), 0)
    cols = lax.broadcasted_iota(jnp.int32, (blk, 2 * blk), 1)

    @pl.when(jnp.logical_and(pl.program_id(0) == 0, n == 0))
    def _():
        dist = jnp.abs(rows + blk - cols).astype(F32)
        dchunk = rows // CHUNK + 2 - cols // CHUNK
        valid = jnp.logical_and(dchunk >= 0, dchunk <= SWA_BLOCK // CHUNK)
        for h, slope in enumerate(_alibi_slopes(SWA_HEADS)):
            bias_ref[h * blk:(h + 1) * blk, :] = jnp.where(valid, (-LOG2E * slope) * dist, -jnp.inf)

    rep = SWA_HEADS // SWA_KV_HEADS
    first = jnp.where(jnp.logical_and(n == 0, cols < blk), -jnp.inf, 0.0)
    first = jnp.concatenate([first] * rep, axis=0)
    lane = lax.broadcasted_iota(jnp.int32, (1, LANES), 1)
    lo = lane < HEAD_DIM
    zero = jnp.zeros((), BF16)
    head_row = lax.broadcasted_iota(jnp.int32, (rep * blk, 1), 0) // blk

    v_both = []
    for kh in range(SWA_KV_HEADS):
        c0 = (kh // 2) * LANES
        k128 = jnp.concatenate([kp_ref[0, :, c0:c0 + LANES], kc_ref[0, :, c0:c0 + LANES]], axis=0)
        v128 = jnp.concatenate([vp_ref[0, :, c0:c0 + LANES], vc_ref[0, :, c0:c0 + LANES]], axis=0)
        k_sw = pltpu.roll(k128, HEAD_DIM, 1)
        v_sw = pltpu.roll(v128, HEAD_DIM, 1)
        k_both = jnp.where(lo, k128, k_sw) if kh % 2 == 0 else jnp.where(lo, k_sw, k128)
        v_both.append(jnp.where(lo, v128, v_sw) if kh % 2 == 0 else jnp.where(lo, v_sw, v128))
        parts = []
        for pr in range(rep // 2):
            pair = kh * (rep // 2) + pr
            q128 = q_ref[0, :, pair * LANES:(pair + 1) * LANES]
            parts += [jnp.where(lo, q128, zero), jnp.where(lo, zero, q128)]
        q4 = jnp.concatenate(parts, axis=0)
        s = lax.dot_general(q4, k_both, NT_DIMS, preferred_element_type=F32)
        s_ref[kh] = s + bias_ref[kh * rep * blk:(kh + 1) * rep * blk, :] + first
    for kh in range(SWA_KV_HEADS):
        sink = jnp.zeros((rep * blk, 1), F32)
        for r in range(rep):
            sink = jnp.where(head_row == r, sink_ref[kh * rep + r], sink)
        sink = sink * LOG2E
        s = s_ref[kh]
        m = jnp.maximum(jnp.max(s, axis=-1, keepdims=True), sink)
        ex = jnp.exp2(s - m)
        den = jnp.sum(ex, axis=-1, keepdims=True) + jnp.exp2(sink - m)
        p_ref[kh] = ex.astype(BF16)
        rden_ref[kh] = jnp.broadcast_to(1.0 / den, (rep * blk, LANES))
    for kh in range(SWA_KV_HEADS):
        o4 = jnp.dot(p_ref[kh], v_both[kh], preferred_element_type=F32) * rden_ref[kh]
        for pr in range(rep // 2):
            pair = kh * (rep // 2) + pr
            out = jnp.where(lo, o4[2 * pr * blk:(2 * pr + 1) * blk], o4[(2 * pr + 1) * blk:(2 * pr + 2) * blk])
            o_ref[0, :, pair * LANES:(pair + 1) * LANES] = out.astype(o_ref.dtype)


def _swa(proj3, sinks, q_col, k_col, v_col):
    b, t, _ = proj3.shape
    blk = SWA_BLOCK
    nb = t // blk
    dq = SWA_HEADS * HEAD_DIM
    dkv = SWA_KV_HEADS * HEAD_DIM
    qb, kb, vb = q_col // dq, k_col // dkv, v_col // dkv
    prev = lambda n: jnp.maximum(n - 1, 0)
    return pl.pallas_call(
        _swa_kernel,
        grid=(b, nb),
        in_specs=[
            pl.BlockSpec(memory_space=pltpu.SMEM),
            pl.BlockSpec((1, blk, dq), lambda i, n: (i, n, qb)),
            pl.BlockSpec((1, blk, dkv), lambda i, n: (i, prev(n), kb)),
            pl.BlockSpec((1, blk, dkv), lambda i, n: (i, n, kb)),
            pl.BlockSpec((1, blk, dkv), lambda i, n: (i, prev(n), vb)),
            pl.BlockSpec((1, blk, dkv), lambda i, n: (i, n, vb)),
        ],
        out_specs=pl.BlockSpec((1, blk, dq), lambda i, n: (i, n, 0)),
        out_shape=jax.ShapeDtypeStruct((b, t, dq), BF16),
        scratch_shapes=[
            pltpu.VMEM((SWA_HEADS * blk, 2 * blk), F32),
            pltpu.VMEM((SWA_KV_HEADS, SWA_HEADS // SWA_KV_HEADS * blk, 2 * blk), F32),
            pltpu.VMEM((SWA_KV_HEADS, SWA_HEADS // SWA_KV_HEADS * blk, 2 * blk), BF16),
            pltpu.VMEM((SWA_KV_HEADS, SWA_HEADS // SWA_KV_HEADS * blk, LANES), F32),
        ],
        compiler_params=_params(("arbitrary", "arbitrary")),
        name="swa",
    )(sinks, proj3, proj3, proj3, proj3, proj3)


def _split_bf16(a):
    hi = a.astype(BF16)
    lo = (a - hi.astype(F32)).astype(BF16)
    return hi, lo


def _ssd_kernel(z_ref, xs_ref, bc_ref, xsp_ref, bcp_ref, dt_ref, cw_ref, cb_ref, dtb_ref, alog_ref, dskip_ref,
                nw_ref, o_ref, st_ref):
    tt = SSD_TILE
    inner = SSD_INNER
    gw = inner // SSD_GROUPS
    ns = SSD_STATE
    t = pl.program_id(1)

    half = tt // 2

    @pl.when(t == 0)
    def _():
        st_ref[...] = jnp.zeros_like(st_ref)

    si = lax.broadcasted_iota(jnp.int32, ((SSD_CONV - 1) * half, tt), 0)
    sc = lax.broadcasted_iota(jnp.int32, ((SSD_CONV - 1) * half, tt), 1)
    sel = (sc == (si % half) + half - (SSD_CONV - 1) + si // half).astype(BF16)

    def conv_silu(cur_ref, prev_ref, c0, c1):
        outs = []
        for hh in range(2):
            cur = cur_ref[0, hh * half:(hh + 1) * half, :]
            if hh == 0:
                before = jnp.where(t > 0, prev_ref[0], jnp.zeros((), BF16))
            else:
                before = cur_ref[0, 0:half, :]
            sh = jnp.dot(sel, jnp.concatenate([before, cur], axis=0), preferred_element_type=F32)
            conv = cb_ref[:, c0:c1] + cw_ref[SSD_CONV - 1:SSD_CONV, c0:c1] * cur.astype(F32)
            for k in range(SSD_CONV - 1):
                conv = conv + cw_ref[k:k + 1, c0:c1] * sh[k * half:(k + 1) * half]
            outs.append(conv * jax.nn.sigmoid(conv))
        return jnp.concatenate(outs, axis=0)

    xs = conv_silu(xs_ref, xsp_ref, 0, inner)
    bcm = conv_silu(bc_ref, bcp_ref, inner, inner + 2 * SSD_GROUPS * ns).astype(BF16)
    bm = bcm[:, 0:SSD_GROUPS * ns]
    cm = bcm[:, SSD_GROUPS * ns:]

    dtx = dt_ref[0] + dtb_ref[...]
    dt_c = jnp.maximum(dtx, 0.0) + jnp.log(1.0 + jnp.exp(-jnp.abs(dtx)))
    adt_c = dt_c * (-jnp.exp(alog_ref[...]))
    er = lax.broadcasted_iota(jnp.int32, (LANES, inner), 0)
    ec = lax.broadcasted_iota(jnp.int32, (LANES, inner), 1)
    expand = (er == ec // HEAD_DIM).astype(BF16)
    d_hi, d_lo = _split_bf16(dt_c)
    dt = jnp.dot(d_hi, expand, preferred_element_type=F32) + jnp.dot(d_lo, expand, preferred_element_type=F32)
    a_hi, a_lo = _split_bf16(adt_c)
    a_hi = jnp.dot(a_hi, expand, preferred_element_type=F32).astype(BF16)
    a_lo = jnp.dot(a_lo, expand, preferred_element_type=F32).astype(BF16)

    ri = lax.broadcasted_iota(jnp.int32, (tt, tt), 0)
    ci = lax.broadcasted_iota(jnp.int32, (tt, tt), 1)
    tril = jnp.logical_and(ri // CHUNK == ci // CHUNK, ci <= ri).astype(BF16)
    rl = lax.broadcasted_iota(jnp.int32, (tt, 1), 0) % CHUNK
    cl = lax.broadcasted_iota(jnp.int32, (1, inner), 1) % HEAD_DIM

    a_cum = jnp.dot(tril, a_hi, preferred_element_type=F32) + jnp.dot(tril, a_lo, preferred_element_type=F32)
    later = rl > cl
    zero_b = jnp.zeros((), BF16)
    seg = (jnp.dot(tril, jnp.where(later, a_hi, zero_b), preferred_element_type=F32)
           + jnp.dot(tril, jnp.where(later, a_lo, zero_b), preferred_element_type=F32))
    decay = jnp.exp(jnp.where(cl <= rl, seg, -jnp.inf))

    xdt = xs * dt
    quad = lax.broadcasted_iota(jnp.int32, (CHUNK, 4 * HEAD_DIM), 1) // HEAD_DIM
    zero = jnp.zeros((), BF16)

    for c in range(tt // CHUNK):
        r0 = c * CHUNK
        rs = slice(r0, r0 + CHUNK)
        acum_c = a_cum[rs]
        a_last = a_cum[r0 + CHUNK - 1:r0 + CHUNK]
        xdt_c = xdt[rs]
        xdt_b = xdt_c.astype(BF16)
        cbs = []
        for g in range(SSD_GROUPS):
            b_g = bm[rs, g * ns:(g + 1) * ns]
            c_g = cm[rs, g * ns:(g + 1) * ns]
            b_tiled = jnp.concatenate([b_g] * (gw // HEAD_DIM), axis=0)
            cbs.append(lax.dot_general(c_g, b_tiled, NT_DIMS, preferred_element_type=F32))
        gmat = (jnp.concatenate(cbs, axis=1) * decay[rs]).astype(BF16)
        yd = []
        for qd in range(inner // (4 * HEAD_DIM)):
            cs = slice(qd * 4 * HEAD_DIM, (qd + 1) * 4 * HEAD_DIM)
            xq = xdt_b[:, cs]
            xbd = jnp.concatenate([jnp.where(quad == jj, xq, zero) for jj in range(4)], axis=0)
            yd.append(jnp.dot(gmat[:, cs], xbd, preferred_element_type=F32))
        y = jnp.concatenate(yd, axis=1)
        din = jnp.exp(acum_c)
        x_end = (xdt_c * jnp.exp(a_last - acum_c)).astype(BF16)
        cdec = jnp.exp(a_last)
        yo = []
        for g in range(SSD_GROUPS):
            gs = slice(g * gw, (g + 1) * gw)
            b_g = bm[rs, g * ns:(g + 1) * ns]
            c_g = cm[rs, g * ns:(g + 1) * ns]
            st_g = st_ref[:, gs]
            yo.append(jnp.dot(c_g, st_g.astype(BF16), preferred_element_type=F32))
            upd = lax.dot_general(b_g, x_end[:, gs], TN_DIMS, preferred_element_type=F32)
            st_ref[:, gs] = st_g * cdec[:, gs] + upd
        y = y + jnp.concatenate(yo, axis=1) * din
        y = y + xs[rs] * dskip_ref[...]
        zz = z_ref[0, rs, :].astype(F32)
        y = y * (zz * jax.nn.sigmoid(zz))
        outs = []
        for g in range(SSD_GROUPS):
            y_g = y[:, g * gw:(g + 1) * gw]
            ms = jnp.mean(y_g * y_g, axis=-1, keepdims=True)
            outs.append(y_g * lax.rsqrt(ms + EPS))
        o_ref[0, rs, :] = (jnp.concatenate(outs, axis=1) * nw_ref[...]).astype(o_ref.dtype)


def _ssd(proj3, dtx3, conv_w, conv_b, dtb, alog, dskip, nw, z_col, xs_col, bc_col):
    b, t, _ = proj3.shape
    tt = SSD_TILE
    assert t % tt == 0
    inner = SSD_INNER
    bcw = 2 * SSD_GROUPS * SSD_STATE
    cdim = inner + bcw
    full = lambda shape: pl.BlockSpec(shape, lambda i, s: (0,) * len(shape))
    return pl.pallas_call(
        _ssd_kernel,
        grid=(b, t // tt),
        in_specs=[
            pl.BlockSpec((1, tt, inner), lambda i, s: (i, s, z_col // inner)),
            pl.BlockSpec((1, tt, inner), lambda i, s: (i, s, xs_col // inner)),
            pl.BlockSpec((1, tt, bcw), lambda i, s: (i, s, bc_col // bcw)),
            pl.BlockSpec((1, tt // 2, inner), lambda i, s: (i, jnp.maximum(2 * s - 1, 0), xs_col // inner)),
            pl.BlockSpec((1, tt // 2, bcw), lambda i, s: (i, jnp.maximum(2 * s - 1, 0), bc_col // bcw)),
            pl.BlockSpec((1, tt, LANES), lambda i, s: (i, s, 0)),
            full((SSD_CONV, cdim)), full((1, cdim)), full((1, LANES)), full((1, LANES)),
            full((1, inner)), full((1, inner)),
        ],
        out_specs=pl.BlockSpec((1, tt, inner), lambda i, s: (i, s, 0)),
        out_shape=jax.ShapeDtypeStruct((b, t, inner), BF16),
        scratch_shapes=[pltpu.VMEM((SSD_STATE, inner), F32)],
        compiler_params=_params(("arbitrary", "arbitrary")),
        name="ssd",
    )(proj3, proj3, proj3, proj3, proj3, dtx3, conv_w, conv_b, dtb, alog, dskip, nw)


def _round_to_bf16(x):
    bits = struct.unpack("<I", struct.pack("<f", x))[0]
    bits = (bits + 0x7FFF + ((bits >> 16) & 1)) & 0xFFFF0000
    return struct.unpack("<f", struct.pack("<I", bits))[0]


def _bf16_terms(x, n):
    out = []
    for _ in range(n):
        out.append(_round_to_bf16(x))
        x -= out[-1]
    return out


LOG2E = math.log2(math.e)
LOG2E_TERMS = _bf16_terms(LOG2E, 3)
DIFF_BIAS_SPLIT = 16


def _diff_attn_kernel(slope_ref, lam_ref, q_ref, k_ref, vt_ref, sn_ref, o_ref, tdiag_ref, kaug_ref,
                      *scratch, lam_init, nq):
    blk = DIFF_BLOCK
    strip = DIFF_STRIP
    ns = DIFF_SCORE_SLOTS
    qm_ref, p_ref, m_ref, l_ref, acc_ref, lfin_ref, accfin_ref = (scratch[2 * i:2 * i + 2] for i in range(7))
    s_ref = [scratch[14 + 2 * i:16 + 2 * i] for i in range(ns)]
    slope = jnp.full((1, 1), slope_ref[pl.program_id(1)], F32)
    kk =lax.broadcasted_iota(jnp.int32, (blk, blk), 0)
    qq = lax.broadcasted_iota(jnp.int32, (blk, blk), 1)
    tdiag_ref[...] = jnp.where(kk // CHUNK > qq // CHUNK, -jnp.inf,
                               jnp.where(kk > qq, (2.0 * LOG2E) * slope * (qq - kk).astype(F32), 0.0))

    lam_p = lam_ref[...]
    lam = (jnp.exp(jnp.sum(lam_p[0:1] * lam_p[1:2], axis=-1, keepdims=True))
           - jnp.exp(jnp.sum(lam_p[2:3] * lam_p[3:4], axis=-1, keepdims=True)) + lam_init)

    lane = lax.broadcasted_iota(jnp.int32, (blk, LANES), 1)
    pos = lax.broadcasted_iota(jnp.int32, (blk, LANES), 0)
    lo = lane < HEAD_DIM
    nt = len(LOG2E_TERMS)
    b_hi = slope * (DIFF_BIAS_SPLIT * (pos // DIFF_BIAS_SPLIT)).astype(F32)
    b_lo = slope * (pos % DIFF_BIAS_SPLIT).astype(F32)
    kbias, qconst = [], []
    for base in (HEAD_DIM, 0):
        rel = lane - base
        kbias.append(jnp.where((rel >= 0) & (rel < nt), b_hi,
                               jnp.where((rel >= nt) & (rel < 2 * nt), b_lo, 0.0)).astype(BF16))
        qc = jnp.zeros((blk, LANES), F32)
        for i, term in enumerate(LOG2E_TERMS):
            qc = jnp.where((rel == i) | (rel == nt + i), term, qc)
        qconst.append(qc.astype(BF16))
    for j in range(nq):
        k = k_ref[0, j * blk:(j + 1) * blk, :]
        kaug_ref[0, j] = jnp.where(lo, k, kbias[0])
        kaug_ref[1, j] = jnp.where(lo, kbias[1], k)

    def scores(slot, br, j, diag):
        s = lax.dot_general(kaug_ref[br, j], qm_ref[br][...], NT_DIMS, preferred_element_type=F32)
        if diag:
            s = s + tdiag_ref[...]
        s_ref[slot][br][...] = s
        return jnp.max(s, axis=0, keepdims=True)

    def exp_strips(br, src, off):
        lp = [None, None]
        for i, r in enumerate(range(0, blk, strip)):
            p = jnp.exp2(src[r:r + strip, :] - off)
            part = p[0:8]
            for g in range(8, strip, 8):
                part = part + p[g:g + 8]
            lp[i % 2] = part if lp[i % 2] is None else lp[i % 2] + part
            p_ref[br][r:r + strip, :] = p.astype(BF16)
        return lp[0] + lp[1]

    def absorb(slot, br, j, mx, shift):
        m_old = m_ref[br][0:1, :]
        m_new = jnp.maximum(m_old, mx + shift)
        alpha = jnp.exp2(m_old - m_new)
        m_ref[br][...] = jnp.broadcast_to(m_new, m_ref[br].shape)
        lp = exp_strips(br, s_ref[slot][br], m_new - shift)
        l_ref[br][...] = alpha * l_ref[br][...] + lp
        pv = jnp.dot(vt_ref[j], p_ref[br][...], preferred_element_type=F32)
        acc_ref[br][...] = alpha * acc_ref[br][...] + pv

    def raw_scores(br, j):
        return lax.dot_general(kaug_ref[br, j], qm_ref[br][...], NT_DIMS, preferred_element_type=F32)

    def absorb_streaming(br, j, s, shift):
        m_old = m_ref[br][0:1, :]
        excess = jnp.max(s, axis=0, keepdims=True) + shift - m_old
        lp = exp_strips(br, s, m_old - shift)
        pv = jnp.dot(vt_ref[j], p_ref[br][...], preferred_element_type=F32)
        up = jnp.maximum(excess, 0.0)
        alpha = jnp.exp2(-up)
        m_ref[br][...] = jnp.broadcast_to(m_old + up, m_ref[br].shape)
        l_ref[br][...] = alpha * (l_ref[br][...] + lp)
        acc_ref[br][...] = alpha * (acc_ref[br][...] + pv)
        return excess

    zero_shift = jnp.zeros((1, 1), F32)

    def shift_of(qi, j):
        return (-LOG2E * blk) * slope * jnp.full((1, 1), qi - j, jnp.int32).astype(F32)

    def reset(br, m_init):
        m_ref[br][...] = jnp.broadcast_to(m_init, m_ref[br].shape)
        l_ref[br][...] = jnp.zeros(l_ref[br].shape, F32)
        acc_ref[br][...] = jnp.zeros(acc_ref[br].shape, F32)

    def load_queries(qi):
        q = q_ref[0, pl.ds(pl.multiple_of(qi * blk, blk), blk), :]
        qm_ref[0][...] = jnp.where(lo, q, qconst[0])
        qm_ref[1][...] = jnp.where(lo, qconst[1], q)

    def write_output(qi, l_src, acc_src):
        r1 = 1.0 / jnp.sum(l_src[0][...], axis=0, keepdims=True)
        r2 = lam / jnp.sum(l_src[1][...], axis=0, keepdims=True)
        o = (acc_src[0][...] * r1 - acc_src[1][...] * r2).T
        ms = jnp.mean(o * o, axis=-1, keepdims=True)
        o = o * lax.rsqrt(ms + EPS) * sn_ref[...] * (1.0 - lam_init)
        q0 = qi * blk if isinstance(qi, int) else pl.multiple_of(qi * blk, blk)
        o_ref[0, pl.ds(q0, blk), :] = o.astype(o_ref.dtype)

    for br in range(2):
        lfin_ref[br][...] = jnp.ones(lfin_ref[br].shape, F32)
        accfin_ref[br][...] = jnp.zeros(accfin_ref[br].shape, F32)

    def q_step(qi, worst):
        write_output(jnp.maximum(qi - 1, 0), lfin_ref, accfin_ref)
        load_queries(qi)
        s = []
        for br in range(2):
            prod = (kaug_ref[br, qi].astype(F32) * qm_ref[br][...].astype(F32)).astype(BF16)
            own = lax.dot_general(jnp.ones((8, LANES), BF16), prod, NT_DIMS, preferred_element_type=F32)
            reset(br, own[0:1, :])
            s.append(raw_scores(br, qi) + tdiag_ref[...])
        worst = jnp.maximum(worst, jnp.maximum(absorb_streaming(0, qi, s[0], zero_shift),
                                               absorb_streaming(1, qi, s[1], zero_shift)))

        def stream_step(j, worst):
            shift = shift_of(qi, j)
            s = [raw_scores(br, j) for br in range(2)]
            e0 = absorb_streaming(0, j, s[0], shift)
            e1 = absorb_streaming(1, j, s[1], shift)
            return jnp.maximum(worst, jnp.maximum(e0, e1))

        nu = DIFF_BLOCKS_PER_TRIP

        def run_blocks(first, count, w):
            for i in range(count):
                w = stream_step(first + i, w)
            return w

        worst = lax.fori_loop(0, qi // nu, lambda t, w: run_blocks(nu * t, nu, w), worst)
        rem = qi % nu
        worst = lax.switch(rem, [functools.partial(run_blocks, qi - rem, r) for r in range(nu)], worst)
        for br in range(2):
            lfin_ref[br][...] = l_ref[br][...]
            accfin_ref[br][...] = acc_ref[br][...]
        return worst

    worst = lax.fori_loop(0, nq, q_step, jnp.full((1, blk), -jnp.inf, F32))
    write_output(nq - 1, lfin_ref, accfin_ref)

    @pl.when(jnp.max(worst) > DIFF_STREAM_HEADROOM)
    def _():
        def q_step_two_pass(qi, _):
            load_queries(qi)
            mx = [scores(0, br, qi, True) for br in range(2)]
            for br in range(2):
                reset(br, -jnp.inf)
                absorb(0, br, qi, mx[br], zero_shift)

            def kv_step(j, _):
                mx = [scores(0, br, j, False) for br in range(2)]
                for br in range(2):
                    absorb(0, br, j, mx[br], shift_of(qi, j))
                return 0

            lax.fori_loop(0, qi, kv_step, 0)
            write_output(qi, l_ref, acc_ref)
            return 0

        lax.fori_loop(0, nq, q_step_two_pass, 0)


def _diff_attn(proj3, vt3, lam_p, sub_norm, lam_init):
    b, t, _ = proj3.shape
    blk = DIFF_BLOCK
    assert t % blk == 0
    hd = 2 * HEAD_DIM
    nh = DIFF_HEADS
    assert all(math.log2(s).is_integer() for s in _alibi_slopes(nh))
    return pl.pallas_call(
        functools.partial(_diff_attn_kernel, lam_init=lam_init, nq=t // blk),
        grid=(b, nh),
        in_specs=[
            pl.BlockSpec(memory_space=pltpu.SMEM),
            pl.BlockSpec((4, HEAD_DIM), lambda i, h: (0, 0)),
            pl.BlockSpec((1, t, hd), lambda i, h: (i, 0, h)),
            pl.BlockSpec((1, t, hd), lambda i, h: (i, 0, nh + h)),
            pl.BlockSpec((t // blk, hd, blk), lambda i, h: (i, h, 0)),
            pl.BlockSpec((1, hd), lambda i, h: (0, 0)),
        ],
        out_specs=pl.BlockSpec((1, t, hd), lambda i, h: (i, 0, h)),
        out_shape=jax.ShapeDtypeStruct((b, t, nh * hd), BF16),
        scratch_shapes=[
            pltpu.VMEM((blk, blk), F32), pltpu.VMEM((2, t // blk, blk, hd), BF16),
            *[pltpu.VMEM((blk, hd), BF16)] * 2, *[pltpu.VMEM((blk, blk), BF16)] * 2,
            *[pltpu.VMEM((8, blk), F32)] * 2, *[pltpu.VMEM((8, blk), F32)] * 2, *[pltpu.VMEM((hd, blk), F32)] * 2,
            *[pltpu.VMEM((8, blk), F32)] * 2, *[pltpu.VMEM((hd, blk), F32)] * 2,
            *[pltpu.VMEM((blk, blk), F32)] * (2 * DIFF_SCORE_SLOTS),
        ],
        compiler_params=_params(("arbitrary", "arbitrary")),
        name="diff_attn",
    )(jnp.asarray(_alibi_slopes(nh), F32), lam_p, proj3, proj3, vt3, sub_norm)


def _group_sum_matrix():
    idx = jnp.arange(GROUP_TILE) // HEAD_DIM
    return (idx[:, None] == idx[None, :]).astype(BF16)


def kernel(x, ev_norm_w, ev_w_in, ev_conv_w, ev_conv_b, ev_dt_bias, ev_a_log, ev_d_skip, ev_ssd_norm_w,
           ev_q_norm, ev_k_norm, ev_sinks, ev_w_out,
           od_norm_w, od_w_in, od_q_norm, od_k_norm, od_lam_q1, od_lam_k1, od_lam_q2, od_lam_k2,
           od_sub_norm, od_w_out, mlp_norm_w, mlp_w1, mlp_w2):
    bsz, t, d = x.shape
    m = bsz * t
    depth = mlp_w1.shape[0]
    gmat = _group_sum_matrix()
    x2 = x.reshape(m, d)
    for layer in range(depth):
        j = layer // 2
        if layer % 2 == 0:
            w_in = ev_w_in[j]
            inner = SSD_INNER
            bcw = 2 * SSD_GROUPS * SSD_STATE
            dq = SWA_HEADS * HEAD_DIM
            dkv = SWA_KV_HEADS * HEAD_DIM
            o_xbc = inner
            o_dt = o_xbc + inner + bcw
            o_q = o_dt + SSD_HEADS
            o_k = o_q + dq
            o_v = o_k + dkv
            w_re = jnp.concatenate([
                w_in[:, 0:inner], w_in[:, o_xbc:o_xbc + inner], w_in[:, o_q:o_q + dq],
                w_in[:, o_xbc + inner:o_dt], w_in[:, o_k:o_k + dkv], w_in[:, o_v:o_v + dkv]], axis=1).astype(BF16)
            z_col, xs_col, q_col, bc_col = 0, inner, 2 * inner, 2 * inner + dq
            k_col = bc_col + bcw
            v_col = k_col + dkv
            n_re = v_col + dkv
            pad_heads = lambda a: jnp.pad(a, [(0, 0)] * (a.ndim - 1) + [(0, LANES - SSD_HEADS)])
            w_dt = pad_heads(w_in[:, o_dt:o_dt + SSD_HEADS]).astype(BF16)
            ones = lambda n: jnp.ones((n,), F32)
            gw = jnp.concatenate([ones(q_col), jnp.tile(ev_q_norm[j], SWA_HEADS) * (LOG2E * HEAD_DIM ** -0.5), ones(bcw),
                                  jnp.tile(ev_k_norm[j], SWA_KV_HEADS), ones(dkv)]).reshape(1, n_re)
            nm = jnp.concatenate([jnp.zeros((q_col,), F32), ones(dq), jnp.zeros((bcw,), F32), ones(dkv),
                                  jnp.zeros((dkv,), F32)]).reshape(1, n_re)
            norm_tiles = sorted({c // GROUP_TILE for c in list(range(q_col, q_col + dq, HEAD_DIM))
                                 + list(range(k_col, k_col + dkv, HEAD_DIM))})
            proj, dtx = _rms_proj(x2, ev_norm_w[j].reshape(1, d), w_re, gw, nm, gmat, norm_tiles, wd=w_dt)
            proj3 = proj.reshape(bsz, t, n_re)
            rep = lambda v: jnp.repeat(v, HEAD_DIM).reshape(1, inner)
            y_ssd = _ssd(proj3, dtx.reshape(bsz, t, LANES), ev_conv_w[j], ev_conv_b[j].reshape(1, -1),
                         pad_heads(ev_dt_bias[j]).reshape(1, LANES), pad_heads(ev_a_log[j]).reshape(1, LANES),
                         rep(ev_d_skip[j]),
                         ev_ssd_norm_w[j].reshape(1, inner), z_col, xs_col, bc_col)
            y_swa = _swa(proj3, ev_sinks[j], q_col, k_col, v_col)
            w_out = ev_w_out[j].astype(BF16)
            ys = [y_ssd.reshape(m, inner), y_swa.reshape(m, dq)]
            wos = [w_out[0:inner], w_out[inner:]]
        else:
            dm = 2 * DIFF_HEADS * HEAD_DIM
            w_qk = od_w_in[j][:, 0:2 * dm].astype(BF16)
            w_v_t = od_w_in[j][:, 2 * dm:].T.astype(BF16)
            gw = jnp.concatenate([jnp.tile(od_q_norm[j], 2 * DIFF_HEADS) * (LOG2E * HEAD_DIM ** -0.5),
                                  jnp.tile(od_k_norm[j], 2 * DIFF_HEADS)]).reshape(1, 2 * dm)
            nm = jnp.ones((1, 2 * dm), F32)
            norm_tiles = list(range(2 * dm // GROUP_TILE))
            proj, vt3 = _rms_proj(x2, od_norm_w[j].reshape(1, d), w_qk, gw, nm, gmat, norm_tiles, wt=w_v_t)
            lam_init = 0.8 - 0.6 * math.exp(-0.3 * layer)
            lam_p = jnp.stack([od_lam_q1[j], od_lam_k1[j], od_lam_q2[j], od_lam_k2[j]])
            y = _diff_attn(proj.reshape(bsz, t, 2 * dm), vt3, lam_p, od_sub_norm[j].reshape(1, -1), lam_init)
            ys = [y.reshape(m, dm)]
            wos = [od_w_out[j].astype(BF16)]
        x2 = _out_mlp(x2, ys, wos, mlp_norm_w[layer].reshape(1, d), mlp_w1[layer].astype(BF16),
                      mlp_w2[layer].astype(BF16))
    return x2.reshape(bsz, t, d)
```

```python
import functools
import math
import struct

import jax
import jax.numpy as jnp
from jax import lax
from jax.experimental import pallas as pl
from jax.experimental.pallas import tpu as pltpu

F32 = jnp.float32
BF16 = jnp.bfloat16

LANES = 128
V7X_VMEM_LIMIT_BYTES = 56 * 1024 * 1024

EPS = 1e-6
CHUNK = 64
HEAD_DIM = 64
GROUP_TILE = 512
SSD_HEADS = 16
SSD_STATE = 128
SSD_GROUPS = 2
SSD_INNER = SSD_HEADS * HEAD_DIM
SSD_CONV = 4
SWA_HEADS = 16
SWA_KV_HEADS = 4
SWA_BLOCK = 128
DIFF_HEADS = 8
DIFF_BLOCK = 512
DIFF_STRIP = 32
DIFF_SCORE_SLOTS = 1
DIFF_STREAM_HEADROOM = 64.0
DIFF_BLOCKS_PER_TRIP = 4
SSD_TILE = 256

NT_DIMS = (((1,), (1,)), ((), ()))
TN_DIMS = (((0,), (0,)), ((), ()))


def _alibi_slopes(n):
    return [2.0 ** (-8.0 * (i + 1) / n) for i in range(n)]


def _params(sem, vmem=V7X_VMEM_LIMIT_BYTES):
    return pltpu.CompilerParams(dimension_semantics=sem, vmem_limit_bytes=vmem)


def _rms_proj_kernel(*refs, norm_tiles, with_dt):
    if with_dt:
        x_ref, nw_ref, w_ref, gw_ref, nm_ref, g_ref, wd_ref, o_ref, dt_ref, h_ref = refs
    else:
        x_ref, nw_ref, w_ref, gw_ref, nm_ref, g_ref, o_ref, h_ref = refs
    tn = GROUP_TILE
    x = x_ref[...]
    ms = jnp.mean(x * x, axis=-1, keepdims=True)
    h_ref[...] = (x * lax.rsqrt(ms + EPS) * nw_ref[...]).astype(BF16)
    if with_dt:
        dt_ref[...] = jnp.dot(h_ref[...], wd_ref[...], preferred_element_type=F32)
    for j in range(w_ref.shape[1] // tn):
        cs = slice(j * tn, (j + 1) * tn)
        acc = jnp.dot(h_ref[...], w_ref[:, cs], preferred_element_type=F32)
        if j in norm_tiles:
            ss = jnp.dot((acc * acc).astype(BF16), g_ref[...], preferred_element_type=F32)
            inv = lax.rsqrt(ss * (1.0 / HEAD_DIM) + EPS)
            acc = acc * (jnp.where(nm_ref[:, cs] > 0.0, inv, 1.0) * gw_ref[:, cs])
        o_ref[:, cs] = acc.astype(o_ref.dtype)


def _resident(shape):
    return pl.BlockSpec(shape, lambda i: (0,) * len(shape), pipeline_mode=pl.Buffered(1))


def _rms_proj(x2d, norm_w, w, gw, nm, gmat, norm_tiles, wd=None, tm=1024):
    m, d = x2d.shape
    n = w.shape[1]
    tm = min(tm, m)
    assert m % tm == 0 and n % GROUP_TILE == 0
    with_dt = wd is not None
    in_specs = [pl.BlockSpec((tm, d), lambda i: (i, 0)), _resident((1, d)), _resident((d, n)),
                _resident((1, n)), _resident((1, n)), _resident(gmat.shape)]
    args = [x2d, norm_w, w, gw, nm, gmat]
    out_shape = [jax.ShapeDtypeStruct((m, n), BF16)]
    out_specs = [pl.BlockSpec((tm, n), lambda i: (i, 0))]
    if with_dt:
        nd = wd.shape[1]
        in_specs.append(_resident((d, nd)))
        args.append(wd)
        out_shape.append(jax.ShapeDtypeStruct((m, nd), F32))
        out_specs.append(pl.BlockSpec((tm, nd), lambda i: (i, 0)))
    res = pl.pallas_call(
        functools.partial(_rms_proj_kernel, norm_tiles=tuple(norm_tiles), with_dt=with_dt),
        grid=(m // tm,),
        in_specs=in_specs,
        out_specs=out_specs,
        out_shape=out_shape,
        scratch_shapes=[pltpu.VMEM((tm, d), BF16)],
        compiler_params=_params(("arbitrary",)),
        name="rms_proj_dt" if with_dt else "rms_proj",
    )(*args)
    return res


def _out_mlp_kernel(*refs, n_mix, tf):
    x_ref = refs[0]
    y_refs = refs[1:1 + n_mix]
    wo_refs = refs[1 + n_mix:1 + 2 * n_mix]
    nw_ref, w1_ref, w2_ref, o_ref, hn_ref = refs[1 + 2 * n_mix:]
    x1 = x_ref[...]
    for y_ref, wo_ref in zip(y_refs, wo_refs):
        x1 = x1 + jnp.dot(y_ref[...], wo_ref[...], preferred_element_type=F32)
    ms = jnp.mean(x1 * x1, axis=-1, keepdims=True)
    hn_ref[...] = (x1 * lax.rsqrt(ms + EPS) * nw_ref[...]).astype(BF16)
    o_ref[...] = x1
    for f in range(w1_ref.shape[1] // tf):
        fs = slice(f * tf, (f + 1) * tf)
        hid = jnp.dot(hn_ref[...], w1_ref[:, fs], preferred_element_type=F32)
        act = jnp.square(jnp.maximum(hid, 0.0)).astype(BF16)
        o_ref[...] += jnp.dot(act, w2_ref[fs, :], preferred_element_type=F32)


def _out_mlp(x2d, ys, wos, norm_w, w1, w2, tm=1024, tf=512):
    m, d = x2d.shape
    dff = w1.shape[1]
    tm = min(tm, m)
    assert m % tm == 0 and dff % tf == 0
    n_mix = len(ys)
    in_specs = [pl.BlockSpec((tm, d), lambda i: (i, 0))]
    in_specs += [pl.BlockSpec((tm, y.shape[1]), lambda i: (i, 0)) for y in ys]
    in_specs += [_resident(wo.shape) for wo in wos]
    in_specs += [_resident((1, d)), _resident(w1.shape), _resident(w2.shape)]
    return pl.pallas_call(
        functools.partial(_out_mlp_kernel, n_mix=n_mix, tf=tf),
        grid=(m // tm,),
        in_specs=in_specs,
        out_specs=pl.BlockSpec((tm, d), lambda i: (i, 0)),
        out_shape=jax.ShapeDtypeStruct((m, d), F32),
        scratch_shapes=[pltpu.VMEM((tm, d), BF16)],
        compiler_params=_params(("arbitrary",)),
        name="out_mlp",
    )(x2d, *ys, *wos, norm_w, w1, w2)


def _swa_kernel(sink_ref, q_ref, kp_ref, kc_ref, vp_ref, vc_ref, o_ref, bias_ref, s_ref, p_ref, rden_ref):
    n = pl.program_id(1)
    blk = SWA_BLOCK
    rows = lax.broadcasted_iota(jnp.int32, (blk, 2 * blk), 0)
    cols = lax.broadcasted_iota(jnp.int32, (blk, 2 * blk), 1)

    @pl.when(jnp.logical_and(pl.program_id(0) == 0, n == 0))
    def _():
        dist = jnp.abs(rows + blk - cols).astype(F32)
        dchunk = rows // CHUNK + 2 - cols // CHUNK
        valid = jnp.logical_and(dchunk >= 0, dchunk <= SWA_BLOCK // CHUNK)
        for h, slope in enumerate(_alibi_slopes(SWA_HEADS)):
            bias_ref[h * blk:(h + 1) * blk, :] = jnp.where(valid, (-LOG2E * slope) * dist, -jnp.inf)

    rep = SWA_HEADS // SWA_KV_HEADS
    first = jnp.where(jnp.logical_and(n == 0, cols < blk), -jnp.inf, 0.0)
    first = jnp.concatenate([first] * rep, axis=0)
    lane = lax.broadcasted_iota(jnp.int32, (1, LANES), 1)
    lo = lane < HEAD_DIM
    zero = jnp.zeros((), BF16)
    head_row = lax.broadcasted_iota(jnp.int32, (rep * blk, 1), 0) // blk

    v_both = []
    for kh in range(SWA_KV_HEADS):
        c0 = (kh // 2) * LANES
        k128 = jnp.concatenate([kp_ref[0, :, c0:c0 + LANES], kc_ref[0, :, c0:c0 + LANES]], axis=0)
        v128 = jnp.concatenate([vp_ref[0, :, c0:c0 + LANES], vc_ref[0, :, c0:c0 + LANES]], axis=0)
        k_sw = pltpu.roll(k128, HEAD_DIM, 1)
        v_sw = pltpu.roll(v128, HEAD_DIM, 1)
        k_both = jnp.where(lo, k128, k_sw) if kh % 2 == 0 else jnp.where(lo, k_sw, k128)
        v_both.append(jnp.where(lo, v128, v_sw) if kh % 2 == 0 else jnp.where(lo, v_sw, v128))
        parts = []
        for pr in range(rep // 2):
            pair = kh * (rep // 2) + pr
            q128 = q_ref[0, :, pair * LANES:(pair + 1) * LANES]
            parts += [jnp.where(lo, q128, zero), jnp.where(lo, zero, q128)]
        q4 = jnp.concatenate(parts, axis=0)
        s = lax.dot_general(q4, k_both, NT_DIMS, preferred_element_type=F32)
        s_ref[kh] = s + bias_ref[kh * rep * blk:(kh + 1) * rep * blk, :] + first
    for kh in range(SWA_KV_HEADS):
        sink = jnp.zeros((rep * blk, 1), F32)
        for r in range(rep):
            sink = jnp.where(head_row == r, sink_ref[kh * rep + r], sink)
        sink = sink * LOG2E
        s = s_ref[kh]
        m = jnp.maximum(jnp.max(s, axis=-1, keepdims=True), sink)
        ex = jnp.exp2(s - m)
        den = jnp.sum(ex, axis=-1, keepdims=True) + jnp.exp2(sink - m)
        p_ref[kh] = ex.astype(BF16)
        rden_ref[kh] = jnp.broadcast_to(1.0 / den, (rep * blk, LANES))
    for kh in range(SWA_KV_HEADS):
        o4 = jnp.dot(p_ref[kh], v_both[kh], preferred_element_type=F32) * rden_ref[kh]
        for pr in range(rep // 2):
            pair = kh * (rep // 2) + pr
            out = jnp.where(lo, o4[2 * pr * blk:(2 * pr + 1) * blk], o4[(2 * pr + 1) * blk:(2 * pr + 2) * blk])
            o_ref[0, :, pair * LANES:(pair + 1) * LANES] = out.astype(o_ref.dtype)


def _swa(proj3, sinks, q_col, k_col, v_col):
    b, t, _ = proj3.shape
    blk = SWA_BLOCK
    nb = t // blk
    dq = SWA_HEADS * HEAD_DIM
    dkv = SWA_KV_HEADS * HEAD_DIM
    qb, kb, vb = q_col // dq, k_col // dkv, v_col // dkv
    prev = lambda n: jnp.maximum(n - 1, 0)
    return pl.pallas_call(
        _swa_kernel,
        grid=(b, nb),
        in_specs=[
            pl.BlockSpec(memory_space=pltpu.SMEM),
            pl.BlockSpec((1, blk, dq), lambda i, n: (i, n, qb)),
            pl.BlockSpec((1, blk, dkv), lambda i, n: (i, prev(n), kb)),
            pl.BlockSpec((1, blk, dkv), lambda i, n: (i, n, kb)),
            pl.BlockSpec((1, blk, dkv), lambda i, n: (i, prev(n), vb)),
            pl.BlockSpec((1, blk, dkv), lambda i, n: (i, n, vb)),
        ],
        out_specs=pl.BlockSpec((1, blk, dq), lambda i, n: (i, n, 0)),
        out_shape=jax.ShapeDtypeStruct((b, t, dq), BF16),
        scratch_shapes=[
            pltpu.VMEM((SWA_HEADS * blk, 2 * blk), F32),
            pltpu.VMEM((SWA_KV_HEADS, SWA_HEADS // SWA_KV_HEADS * blk, 2 * blk), F32),
            pltpu.VMEM((SWA_KV_HEADS, SWA_HEADS // SWA_KV_HEADS * blk, 2 * blk), BF16),
            pltpu.VMEM((SWA_KV_HEADS, SWA_HEADS // SWA_KV_HEADS * blk, LANES), F32),
        ],
        compiler_params=_params(("arbitrary", "arbitrary")),
        name="swa",
    )(sinks, proj3, proj3, proj3, proj3, proj3)


def _split_bf16(a):
    hi = a.astype(BF16)
    lo = (a - hi.astype(F32)).astype(BF16)
    return hi, lo


def _ssd_kernel(z_ref, xs_ref, bc_ref, xsp_ref, bcp_ref, dt_ref, cw_ref, cb_ref, dtb_ref, alog_ref, dskip_ref,
                nw_ref, o_ref, st_ref):
    tt = SSD_TILE
    inner = SSD_INNER
    gw = inner // SSD_GROUPS
    ns = SSD_STATE
    t = pl.program_id(1)

    half = tt // 2

    @pl.when(t == 0)
    def _():
        st_ref[...] = jnp.zeros_like(st_ref)

    si = lax.broadcasted_iota(jnp.int32, ((SSD_CONV - 1) * half, tt), 0)
    sc = lax.broadcasted_iota(jnp.int32, ((SSD_CONV - 1) * half, tt), 1)
    sel = (sc == (si % half) + half - (SSD_CONV - 1) + si // half).astype(BF16)

    def conv_silu(cur_ref, prev_ref, c0, c1):
        outs = []
        for hh in range(2):
            cur = cur_ref[0, hh * half:(hh + 1) * half, :]
            if hh == 0:
                before = jnp.where(t > 0, prev_ref[0], jnp.zeros((), BF16))
            else:
                before = cur_ref[0, 0:half, :]
            sh = jnp.dot(sel, jnp.concatenate([before, cur], axis=0), preferred_element_type=F32)
            conv = cb_ref[:, c0:c1] + cw_ref[SSD_CONV - 1:SSD_CONV, c0:c1] * cur.astype(F32)
            for k in range(SSD_CONV - 1):
                conv = conv + cw_ref[k:k + 1, c0:c1] * sh[k * half:(k + 1) * half]
            outs.append(conv * jax.nn.sigmoid(conv))
        return jnp.concatenate(outs, axis=0)

    xs = conv_silu(xs_ref, xsp_ref, 0, inner)
    bcm = conv_silu(bc_ref, bcp_ref, inner, inner + 2 * SSD_GROUPS * ns).astype(BF16)
    bm = bcm[:, 0:SSD_GROUPS * ns]
    cm = bcm[:, SSD_GROUPS * ns:]

    dtx = dt_ref[0] + dtb_ref[...]
    dt_c = jnp.maximum(dtx, 0.0) + jnp.log(1.0 + jnp.exp(-jnp.abs(dtx)))
    adt_c = dt_c * (-jnp.exp(alog_ref[...]))
    er = lax.broadcasted_iota(jnp.int32, (LANES, inner), 0)
    ec = lax.broadcasted_iota(jnp.int32, (LANES, inner), 1)
    expand = (er == ec // HEAD_DIM).astype(BF16)
    d_hi, d_lo = _split_bf16(dt_c)
    dt = jnp.dot(d_hi, expand, preferred_element_type=F32) + jnp.dot(d_lo, expand, preferred_element_type=F32)
    a_hi, a_lo = _split_bf16(adt_c)
    a_hi = jnp.dot(a_hi, expand, preferred_element_type=F32).astype(BF16)
    a_lo = jnp.dot(a_lo, expand, preferred_element_type=F32).astype(BF16)

    ri = lax.broadcasted_iota(jnp.int32, (tt, tt), 0)
    ci = lax.broadcasted_iota(jnp.int32, (tt, tt), 1)
    tril = jnp.logical_and(ri // CHUNK == ci // CHUNK, ci <= ri).astype(BF16)
    rl = lax.broadcasted_iota(jnp.int32, (tt, 1), 0) % CHUNK
    cl = lax.broadcasted_iota(jnp.int32, (1, inner), 1) % HEAD_DIM

    a_cum = jnp.dot(tril, a_hi, preferred_element_type=F32) + jnp.dot(tril, a_lo, preferred_element_type=F32)
    later = rl > cl
    zero_b = jnp.zeros((), BF16)
    seg = (jnp.dot(tril, jnp.where(later, a_hi, zero_b), preferred_element_type=F32)
           + jnp.dot(tril, jnp.where(later, a_lo, zero_b), preferred_element_type=F32))
    decay = jnp.exp(jnp.where(cl <= rl, seg, -jnp.inf))

    xdt = xs * dt
    quad = lax.broadcasted_iota(jnp.int32, (CHUNK, 4 * HEAD_DIM), 1) // HEAD_DIM
    zero = jnp.zeros((), BF16)

    for c in range(tt // CHUNK):
        r0 = c * CHUNK
        rs = slice(r0, r0 + CHUNK)
        acum_c = a_cum[rs]
        a_last = a_cum[r0 + CHUNK - 1:r0 + CHUNK]
        xdt_c = xdt[rs]
        xdt_b = xdt_c.astype(BF16)
        cbs = []
        for g in range(SSD_GROUPS):
            b_g = bm[rs, g * ns:(g + 1) * ns]
            c_g = cm[rs, g * ns:(g + 1) * ns]
            b_tiled = jnp.concatenate([b_g] * (gw // HEAD_DIM), axis=0)
            cbs.append(lax.dot_general(c_g, b_tiled, NT_DIMS, preferred_element_type=F32))
        gmat = (jnp.concatenate(cbs, axis=1) * decay[rs]).astype(BF16)
        yd = []
        for qd in range(inner // (4 * HEAD_DIM)):
            cs = slice(qd * 4 * HEAD_DIM, (qd + 1) * 4 * HEAD_DIM)
            xq = xdt_b[:, cs]
            xbd = jnp.concatenate([jnp.where(quad == jj, xq, zero) for jj in range(4)], axis=0)
            yd.append(jnp.dot(gmat[:, cs], xbd, preferred_element_type=F32))
        y = jnp.concatenate(yd, axis=1)
        din = jnp.exp(acum_c)
        x_end = (xdt_c * jnp.exp(a_last - acum_c)).astype(BF16)
        cdec = jnp.exp(a_last)
        yo = []
        for g in range(SSD_GROUPS):
            gs = slice(g * gw, (g + 1) * gw)
            b_g = bm[rs, g * ns:(g + 1) * ns]
            c_g = cm[rs, g * ns:(g + 1) * ns]
            st_g = st_ref[:, gs]
            yo.append(jnp.dot(c_g, st_g.astype(BF16), preferred_element_type=F32))
            upd = lax.dot_general(b_g, x_end[:, gs], TN_DIMS, preferred_element_type=F32)
            st_ref[:, gs] = st_g * cdec[:, gs] + upd
        y = y + jnp.concatenate(yo, axis=1) * din
        y = y + xs[rs] * dskip_ref[...]
        zz = z_ref[0, rs, :].astype(F32)
        y = y * (zz * jax.nn.sigmoid(zz))
        outs = []
        for g in range(SSD_GROUPS):
            y_g = y[:, g * gw:(g + 1) * gw]
            ms = jnp.mean(y_g * y_g, axis=-1, keepdims=True)
            outs.append(y_g * lax.rsqrt(ms + EPS))
        o_ref[0, rs, :] = (jnp.concatenate(outs, axis=1) * nw_ref[...]).astype(o_ref.dtype)


def _ssd(proj3, dtx3, conv_w, conv_b, dtb, alog, dskip, nw, z_col, xs_col, bc_col):
    b, t, _ = proj3.shape
    tt = SSD_TILE
    assert t % tt == 0
    inner = SSD_INNER
    bcw = 2 * SSD_GROUPS * SSD_STATE
    cdim = inner + bcw
    full = lambda shape: pl.BlockSpec(shape, lambda i, s: (0,) * len(shape))
    return pl.pallas_call(
        _ssd_kernel,
        grid=(b, t // tt),
        in_specs=[
            pl.BlockSpec((1, tt, inner), lambda i, s: (i, s, z_col // inner)),
            pl.BlockSpec((1, tt, inner), lambda i, s: (i, s, xs_col // inner)),
            pl.BlockSpec((1, tt, bcw), lambda i, s: (i, s, bc_col // bcw)),
            pl.BlockSpec((1, tt // 2, inner), lambda i, s: (i, jnp.maximum(2 * s - 1, 0), xs_col // inner)),
            pl.BlockSpec((1, tt // 2, bcw), lambda i, s: (i, jnp.maximum(2 * s - 1, 0), bc_col // bcw)),
            pl.BlockSpec((1, tt, LANES), lambda i, s: (i, s, 0)),
            full((SSD_CONV, cdim)), full((1, cdim)), full((1, LANES)), full((1, LANES)),
            full((1, inner)), full((1, inner)),
        ],
        out_specs=pl.BlockSpec((1, tt, inner), lambda i, s: (i, s, 0)),
        out_shape=jax.ShapeDtypeStruct((b, t, inner), BF16),
        scratch_shapes=[pltpu.VMEM((SSD_STATE, inner), F32)],
        compiler_params=_params(("arbitrary", "arbitrary")),
        name="ssd",
    )(proj3, proj3, proj3, proj3, proj3, dtx3, conv_w, conv_b, dtb, alog, dskip, nw)


def _round_to_bf16(x):
    bits = struct.unpack("<I", struct.pack("<f", x))[0]
    bits = (bits + 0x7FFF + ((bits >> 16) & 1)) & 0xFFFF0000
    return struct.unpack("<f", struct.pack("<I", bits))[0]


def _bf16_terms(x, n):
    out = []
    for _ in range(n):
        out.append(_round_to_bf16(x))
        x -= out[-1]
    return out


LOG2E = math.log2(math.e)
LOG2E_TERMS = _bf16_terms(LOG2E, 3)
DIFF_BIAS_SPLIT = 16


def _diff_attn_kernel(slope_ref, lam_ref, q_ref, k_ref, v_ref, sn_ref, o_ref, tdiag_ref, kaug_ref, vt_ref,
                      *scratch, lam_init, nq):
    blk = DIFF_BLOCK
    strip = DIFF_STRIP
    ns = DIFF_SCORE_SLOTS
    qm_ref, p_ref, m_ref, l_ref, acc_ref, lfin_ref, accfin_ref = (scratch[2 * i:2 * i + 2] for i in range(7))
    s_ref = [scratch[14 + 2 * i:16 + 2 * i] for i in range(ns)]
    slope = jnp.full((1, 1), slope_ref[pl.program_id(1)], F32)
    kk = lax.broadcasted_iota(jnp.int32, (blk, blk), 0)
    qq = lax.broadcasted_iota(jnp.int32, (blk, blk), 1)
    tdiag_ref[...] = jnp.where(kk // CHUNK > qq // CHUNK, -jnp.inf,
                               jnp.where(kk > qq, (2.0 * LOG2E) * slope * (qq - kk).astype(F32), 0.0))

    lam_p = lam_ref[...]
    lam = (jnp.exp(jnp.sum(lam_p[0:1] * lam_p[1:2], axis=-1, keepdims=True))
           - jnp.exp(jnp.sum(lam_p[2:3] * lam_p[3:4], axis=-1, keepdims=True)) + lam_init)

    lane = lax.broadcasted_iota(jnp.int32, (blk, LANES), 1)
    pos = lax.broadcasted_iota(jnp.int32, (blk, LANES), 0)
    lo = lane < HEAD_DIM
    nt = len(LOG2E_TERMS)
    b_hi = slope * (DIFF_BIAS_SPLIT * (pos // DIFF_BIAS_SPLIT)).astype(F32)
    b_lo = slope * (pos % DIFF_BIAS_SPLIT).astype(F32)
    kbias, qconst = [], []
    for base in (HEAD_DIM, 0):
        rel = lane - base
        kbias.append(jnp.where((rel >= 0) & (rel < nt), b_hi,
                               jnp.where((rel >= nt) & (rel < 2 * nt), b_lo, 0.0)).astype(BF16))
        qc = jnp.zeros((blk, LANES), F32)
        for i, term in enumerate(LOG2E_TERMS):
            qc = jnp.where((rel == i) | (rel == nt + i), term, qc)
        qconst.append(qc.astype(BF16))
    for j in range(nq):
        k = k_ref[0, j * blk:(j + 1) * blk, :]
        kaug_ref[0, j] = jnp.where(lo, k, kbias[0])
        kaug_ref[1, j] = jnp.where(lo, kbias[1], k)
        vt_ref[j] = v_ref[0, j * blk:(j + 1) * blk, :].astype(F32).T.astype(BF16)

    def scores(slot, br, j, diag):
        s = lax.dot_general(kaug_ref[br, j], qm_ref[br][...], NT_DIMS, preferred_element_type=F32)
        if diag:
            s = s + tdiag_ref[...]
        s_ref[slot][br][...] = s
        return jnp.max(s, axis=0, keepdims=True)

    def exp_strips(br, src, off):
        lp = [None, None]
        for i, r in enumerate(range(0, blk, strip)):
            p = jnp.exp2(src[r:r + strip, :] - off)
            part = p[0:8]
            for g in range(8, strip, 8):
                part = part + p[g:g + 8]
            lp[i % 2] = part if lp[i % 2] is None else lp[i % 2] + part
            p_ref[br][r:r + strip, :] = p.astype(BF16)
        return lp[0] + lp[1]

    def absorb(slot, br, j, mx, shift):
        m_old = m_ref[br][0:1, :]
        m_new = jnp.maximum(m_old, mx + shift)
        alpha = jnp.exp2(m_old - m_new)
        m_ref[br][...] = jnp.broadcast_to(m_new, m_ref[br].shape)
        lp = exp_strips(br, s_ref[slot][br], m_new - shift)
        l_ref[br][...] = alpha * l_ref[br][...] + lp
        pv = jnp.dot(vt_ref[j], p_ref[br][...], preferred_element_type=F32)
        acc_ref[br][...] = alpha * acc_ref[br][...] + pv

    def raw_scores(br, j):
        return lax.dot_general(kaug_ref[br, j], qm_ref[br][...], NT_DIMS, preferred_element_type=F32)

    def absorb_streaming(br, j, s, shift):
        m_old = m_ref[br][0:1, :]
        excess = jnp.max(s, axis=0, keepdims=True) + shift - m_old
        lp = exp_strips(br, s, m_old - shift)
        pv = jnp.dot(vt_ref[j], p_ref[br][...], preferred_element_type=F32)
        up = jnp.maximum(excess, 0.0)
        alpha = jnp.exp2(-up)
        m_ref[br][...] = jnp.broadcast_to(m_old + up, m_ref[br].shape)
        l_ref[br][...] = alpha * (l_ref[br][...] + lp)
        acc_ref[br][...] = alpha * (acc_ref[br][...] + pv)
        return excess

    zero_shift = jnp.zeros((1, 1), F32)

    def shift_of(qi, j):
        return (-LOG2E * blk) * slope * jnp.full((1, 1), qi - j, jnp.int32).astype(F32)

    def reset(br, m_init):
        m_ref[br][...] = jnp.broadcast_to(m_init, m_ref[br].shape)
        l_ref[br][...] = jnp.zeros(l_ref[br].shape, F32)
        acc_ref[br][...] = jnp.zeros(acc_ref[br].shape, F32)

    def load_queries(qi):
        q = q_ref[0, pl.ds(pl.multiple_of(qi * blk, blk), blk), :]
        qm_ref[0][...] = jnp.where(lo, q, qconst[0])
        qm_ref[1][...] = jnp.where(lo, qconst[1], q)

    def write_output(qi, l_src, acc_src):
        r1 = 1.0 / jnp.sum(l_src[0][...], axis=0, keepdims=True)
        r2 = lam / jnp.sum(l_src[1][...], axis=0, keepdims=True)
        o = (acc_src[0][...] * r1 - acc_src[1][...] * r2).T
        ms = jnp.mean(o * o, axis=-1, keepdims=True)
        o = o * lax.rsqrt(ms + EPS) * sn_ref[...] * (1.0 - lam_init)
        q0 = qi * blk if isinstance(qi, int) else pl.multiple_of(qi * blk, blk)
        o_ref[0, pl.ds(q0, blk), :] = o.astype(o_ref.dtype)

    for br in range(2):
        lfin_ref[br][...] = jnp.ones(lfin_ref[br].shape, F32)
        accfin_ref[br][...] = jnp.zeros(accfin_ref[br].shape, F32)

    def q_step(qi, worst):
        write_output(jnp.maximum(qi - 1, 0), lfin_ref, accfin_ref)
        load_queries(qi)
        s = []
        for br in range(2):
            prod = (kaug_ref[br, qi].astype(F32) * qm_ref[br][...].astype(F32)).astype(BF16)
            own = lax.dot_general(jnp.ones((8, LANES), BF16), prod, NT_DIMS, preferred_element_type=F32)
            reset(br, own[0:1, :])
            s.append(raw_scores(br, qi) + tdiag_ref[...])
        worst = jnp.maximum(worst, jnp.maximum(absorb_streaming(0, qi, s[0], zero_shift),
                                               absorb_streaming(1, qi, s[1], zero_shift)))

        def stream_step(j, worst):
            shift = shift_of(qi, j)
            s = [raw_scores(br, j) for br in range(2)]
            e0 = absorb_streaming(0, j, s[0], shift)
            e1 = absorb_streaming(1, j, s[1], shift)
            return jnp.maximum(worst, jnp.maximum(e0, e1))

        nu = DIFF_BLOCKS_PER_TRIP

        def run_blocks(first, count, w):
            for i in range(count):
                w = stream_step(first + i, w)
            return w

        worst = lax.fori_loop(0, qi // nu, lambda t, w: run_blocks(nu * t, nu, w), worst)
        rem = qi % nu
        worst = lax.switch(rem, [functools.partial(run_blocks, qi - rem, r) for r in range(nu)], worst)
        for br in range(2):
            lfin_ref[br][...] = l_ref[br][...]
            accfin_ref[br][...] = acc_ref[br][...]
        return worst

    worst = lax.fori_loop(0, nq, q_step, jnp.full((1, blk), -jnp.inf, F32))
    write_output(nq - 1, lfin_ref, accfin_ref)

    @pl.when(jnp.max(worst) > DIFF_STREAM_HEADROOM)
    def _():
        def q_step_two_pass(qi, _):
            load_queries(qi)
            mx = [scores(0, br, qi, True) for br in range(2)]
            for br in range(2):
                reset(br, -jnp.inf)
                absorb(0, br, qi, mx[br], zero_shift)

            def kv_step(j, _):
                mx = [scores(0, br, j, False) for br in range(2)]
                for br in range(2):
                    absorb(0, br, j, mx[br], shift_of(qi, j))
                return 0

            lax.fori_loop(0, qi, kv_step, 0)
            write_output(qi, l_ref, acc_ref)
            return 0

        lax.fori_loop(0, nq, q_step_two_pass, 0)


def _diff_attn(proj3, lam_p, sub_norm, lam_init):
    b, t, _ = proj3.shape
    blk = DIFF_BLOCK
    assert t % blk == 0
    hd = 2 * HEAD_DIM
    nh = DIFF_HEADS
    assert all(math.log2(s).is_integer() for s in _alibi_slopes(nh))
    return pl.pallas_call(
        functools.partial(_diff_attn_kernel, lam_init=lam_init, nq=t // blk),
        grid=(b, nh),
        in_specs=[
            pl.BlockSpec(memory_space=pltpu.SMEM),
            pl.BlockSpec((4, HEAD_DIM), lambda i, h: (0, 0)),
            pl.BlockSpec((1, t, hd), lambda i, h: (i, 0, h)),
            pl.BlockSpec((1, t, hd), lambda i, h: (i, 0, nh + h)),
            pl.BlockSpec((1, t, hd), lambda i, h: (i, 0, 2 * nh + h)),
            pl.BlockSpec((1, hd), lambda i, h: (0, 0)),
        ],
        out_specs=pl.BlockSpec((1, t, hd), lambda i, h: (i, 0, h)),
        out_shape=jax.ShapeDtypeStruct((b, t, nh * hd), BF16),
        scratch_shapes=[
            pltpu.VMEM((blk, blk), F32), pltpu.VMEM((2, t // blk, blk, hd), BF16),
            pltpu.VMEM((t // blk, hd, blk), BF16),
            *[pltpu.VMEM((blk, hd), BF16)] * 2, *[pltpu.VMEM((blk, blk), BF16)] * 2,
            *[pltpu.VMEM((8, blk), F32)] * 2, *[pltpu.VMEM((8, blk), F32)] * 2, *[pltpu.VMEM((hd, blk), F32)] * 2,
            *[pltpu.VMEM((8, blk), F32)] * 2, *[pltpu.VMEM((hd, blk), F32)] * 2,
            *[pltpu.VMEM((blk, blk), F32)] * (2 * DIFF_SCORE_SLOTS),
        ],
        compiler_params=_params(("arbitrary", "arbitrary")),
        name="diff_attn",
    )(jnp.asarray(_alibi_slopes(nh), F32), lam_p, proj3, proj3, proj3, sub_norm)


def _group_sum_matrix():
    idx = jnp.arange(GROUP_TILE) // HEAD_DIM
    return (idx[:, None] == idx[None, :]).astype(BF16)


def kernel(x, ev_norm_w, ev_w_in, ev_conv_w, ev_conv_b, ev_dt_bias, ev_a_log, ev_d_skip, ev_ssd_norm_w,
           ev_q_norm, ev_k_norm, ev_sinks, ev_w_out,
           od_norm_w, od_w_in, od_q_norm, od_k_norm, od_lam_q1, od_lam_k1, od_lam_q2, od_lam_k2,
           od_sub_norm, od_w_out, mlp_norm_w, mlp_w1, mlp_w2):
    bsz, t, d = x.shape
    m = bsz * t
    depth = mlp_w1.shape[0]
    gmat = _group_sum_matrix()
    x2 = x.reshape(m, d)
    for layer in range(depth):
        j = layer // 2
        if layer % 2 == 0:
            w_in = ev_w_in[j]
            inner = SSD_INNER
            bcw = 2 * SSD_GROUPS * SSD_STATE
            dq = SWA_HEADS * HEAD_DIM
            dkv = SWA_KV_HEADS * HEAD_DIM
            o_xbc = inner
            o_dt = o_xbc + inner + bcw
            o_q = o_dt + SSD_HEADS
            o_k = o_q + dq
            o_v = o_k + dkv
            w_re = jnp.concatenate([
                w_in[:, 0:inner], w_in[:, o_xbc:o_xbc + inner], w_in[:, o_q:o_q + dq],
                w_in[:, o_xbc + inner:o_dt], w_in[:, o_k:o_k + dkv], w_in[:, o_v:o_v + dkv]], axis=1).astype(BF16)
            z_col, xs_col, q_col, bc_col = 0, inner, 2 * inner, 2 * inner + dq
            k_col = bc_col + bcw
            v_col = k_col + dkv
            n_re = v_col + dkv
            pad_heads = lambda a: jnp.pad(a, [(0, 0)] * (a.ndim - 1) + [(0, LANES - SSD_HEADS)])
            w_dt = pad_heads(w_in[:, o_dt:o_dt + SSD_HEADS]).astype(BF16)
            ones = lambda n: jnp.ones((n,), F32)
            gw = jnp.concatenate([ones(q_col), jnp.tile(ev_q_norm[j], SWA_HEADS) * (LOG2E * HEAD_DIM ** -0.5), ones(bcw),
                                  jnp.tile(ev_k_norm[j], SWA_KV_HEADS), ones(dkv)]).reshape(1, n_re)
            nm = jnp.concatenate([jnp.zeros((q_col,), F32), ones(dq), jnp.zeros((bcw,), F32), ones(dkv),
                                  jnp.zeros((dkv,), F32)]).reshape(1, n_re)
            norm_tiles = sorted({c // GROUP_TILE for c in list(range(q_col, q_col + dq, HEAD_DIM))
                                 + list(range(k_col, k_col + dkv, HEAD_DIM))})
            proj, dtx = _rms_proj(x2, ev_norm_w[j].reshape(1, d), w_re, gw, nm, gmat, norm_tiles, wd=w_dt)
            proj3 = proj.reshape(bsz, t, n_re)
            rep = lambda v: jnp.repeat(v, HEAD_DIM).reshape(1, inner)
            y_ssd = _ssd(proj3, dtx.reshape(bsz, t, LANES), ev_conv_w[j], ev_conv_b[j].reshape(1, -1),
                         pad_heads(ev_dt_bias[j]).reshape(1, LANES), pad_heads(ev_a_log[j]).reshape(1, LANES),
                         rep(ev_d_skip[j]),
                         ev_ssd_norm_w[j].reshape(1, inner), z_col, xs_col, bc_col)
            y_swa = _swa(proj3, ev_sinks[j], q_col, k_col, v_col)
            w_out = ev_w_out[j].astype(BF16)
            ys = [y_ssd.reshape(m, inner), y_swa.reshape(m, dq)]
            wos = [w_out[0:inner], w_out[inner:]]
        else:
            w_in = od_w_in[j].astype(BF16)
            dm = 2 * DIFF_HEADS * HEAD_DIM
            ones = jnp.ones((dm,), F32)
            gw = jnp.concatenate([jnp.tile(od_q_norm[j], 2 * DIFF_HEADS) * (LOG2E * HEAD_DIM ** -0.5),
                                  jnp.tile(od_k_norm[j], 2 * DIFF_HEADS), ones]).reshape(1, 3 * dm)
            nm = jnp.concatenate([ones, ones, jnp.zeros((dm,), F32)]).reshape(1, 3 * dm)
            norm_tiles = list(range(2 * dm // GROUP_TILE))
            (proj,) = _rms_proj(x2, od_norm_w[j].reshape(1, d), w_in, gw, nm, gmat, norm_tiles)
            lam_init = 0.8 - 0.6 * math.exp(-0.3 * layer)
            lam_p = jnp.stack([od_lam_q1[j], od_lam_k1[j], od_lam_q2[j], od_lam_k2[j]])
            y = _diff_attn(proj.reshape(bsz, t, 3 * dm), lam_p, od_sub_norm[j].reshape(1, -1), lam_init)
            ys = [y.reshape(m, dm)]
            wos = [od_w_out[j].astype(BF16)]
        x2 = _out_mlp(x2, ys, wos, mlp_norm_w[layer].reshape(1, d), mlp_w1[layer].astype(BF16),
                      mlp_w2[layer].astype(BF16))
    return x2.reshape(bsz, t, d)
```

```python
import functools
import math
import struct

import jax
import jax.numpy as jnp
from jax import lax
from jax.experimental import pallas as pl
from jax.experimental.pallas import tpu as pltpu

F32 = jnp.float32
BF16 = jnp.bfloat16

LANES = 128
V7X_VMEM_LIMIT_BYTES = 56 * 1024 * 1024

EPS = 1e-6
CHUNK = 64
HEAD_DIM = 64
GROUP_TILE = 512
SSD_HEADS = 16
SSD_STATE = 128
SSD_GROUPS = 2
SSD_INNER = SSD_HEADS * HEAD_DIM
SSD_CONV = 4
SWA_HEADS = 16
SWA_KV_HEADS = 4
SWA_BLOCK = 128
DIFF_HEADS = 8
DIFF_BLOCK = 512
DIFF_STRIP = 32
DIFF_SCORE_SLOTS = 1
DIFF_STREAM_HEADROOM = 64.0
DIFF_BLOCKS_PER_TRIP = 4
SSD_TILE = 256

NT_DIMS = (((1,), (1,)), ((), ()))
TN_DIMS = (((0,), (0,)), ((), ()))


def _alibi_slopes(n):
    return [2.0 ** (-8.0 * (i + 1) / n) for i in range(n)]


def _params(sem, vmem=V7X_VMEM_LIMIT_BYTES):
    return pltpu.CompilerParams(dimension_semantics=sem, vmem_limit_bytes=vmem)


def _rms_proj_kernel(*refs, norm_tiles, with_dt):
    if with_dt:
        x_ref, nw_ref, w_ref, gw_ref, nm_ref, g_ref, wd_ref, o_ref, dt_ref, h_ref = refs
    else:
        x_ref, nw_ref, w_ref, gw_ref, nm_ref, g_ref, o_ref, h_ref = refs
    tn = GROUP_TILE
    x = x_ref[...]
    ms = jnp.mean(x * x, axis=-1, keepdims=True)
    h_ref[...] = (x * lax.rsqrt(ms + EPS) * nw_ref[...]).astype(BF16)
    if with_dt:
        dt_ref[...] = jnp.dot(h_ref[...], wd_ref[...], preferred_element_type=F32)
    for j in range(w_ref.shape[1] // tn):
        cs = slice(j * tn, (j + 1) * tn)
        acc = jnp.dot(h_ref[...], w_ref[:, cs], preferred_element_type=F32)
        if j in norm_tiles:
            ss = jnp.dot((acc * acc).astype(BF16), g_ref[...], preferred_element_type=F32)
            inv = lax.rsqrt(ss * (1.0 / HEAD_DIM) + EPS)
            acc = acc * (jnp.where(nm_ref[:, cs] > 0.0, inv, 1.0) * gw_ref[:, cs])
        o_ref[:, cs] = acc.astype(o_ref.dtype)


def _resident(shape):
    return pl.BlockSpec(shape, lambda i: (0,) * len(shape), pipeline_mode=pl.Buffered(1))


def _rms_proj(x2d, norm_w, w, gw, nm, gmat, norm_tiles, wd=None, tm=1024):
    m, d = x2d.shape
    n = w.shape[1]
    tm = min(tm, m)
    assert m % tm == 0 and n % GROUP_TILE == 0
    with_dt = wd is not None
    in_specs = [pl.BlockSpec((tm, d), lambda i: (i, 0)), _resident((1, d)), _resident((d, n)),
                _resident((1, n)), _resident((1, n)), _resident(gmat.shape)]
    args = [x2d, norm_w, w, gw, nm, gmat]
    out_shape = [jax.ShapeDtypeStruct((m, n), BF16)]
    out_specs = [pl.BlockSpec((tm, n), lambda i: (i, 0))]
    if with_dt:
        nd = wd.shape[1]
        in_specs.append(_resident((d, nd)))
        args.append(wd)
        out_shape.append(jax.ShapeDtypeStruct((m, nd), F32))
        out_specs.append(pl.BlockSpec((tm, nd), lambda i: (i, 0)))
    res = pl.pallas_call(
        functools.partial(_rms_proj_kernel, norm_tiles=tuple(norm_tiles), with_dt=with_dt),
        grid=(m // tm,),
        in_specs=in_specs,
        out_specs=out_specs,
        out_shape=out_shape,
        scratch_shapes=[pltpu.VMEM((tm, d), BF16)],
        compiler_params=_params(("arbitrary",)),
        name="rms_proj_dt" if with_dt else "rms_proj",
    )(*args)
    return res


def _out_mlp_kernel(*refs, n_mix, tf):
    x_ref = refs[0]
    y_refs = refs[1:1 + n_mix]
    wo_refs = refs[1 + n_mix:1 + 2 * n_mix]
    nw_ref, w1_ref, w2_ref, o_ref, hn_ref = refs[1 + 2 * n_mix:]
    x1 = x_ref[...]
    for y_ref, wo_ref in zip(y_refs, wo_refs):
        x1 = x1 + jnp.dot(y_ref[...], wo_ref[...], preferred_element_type=F32)
    ms = jnp.mean(x1 * x1, axis=-1, keepdims=True)
    hn_ref[...] = (x1 * lax.rsqrt(ms + EPS) * nw_ref[...]).astype(BF16)
    o_ref[...] = x1
    for f in range(w1_ref.shape[1] // tf):
        fs = slice(f * tf, (f + 1) * tf)
        hid = jnp.dot(hn_ref[...], w1_ref[:, fs], preferred_element_type=F32)
        act = jnp.square(jnp.maximum(hid, 0.0)).astype(BF16)
        o_ref[...] += jnp.dot(act, w2_ref[fs, :], preferred_element_type=F32)


def _out_mlp(x2d, ys, wos, norm_w, w1, w2, tm=1024, tf=512):
    m, d = x2d.shape
    dff = w1.shape[1]
    tm = min(tm, m)
    assert m % tm == 0 and dff % tf == 0
    n_mix = len(ys)
    in_specs = [pl.BlockSpec((tm, d), lambda i: (i, 0))]
    in_specs += [pl.BlockSpec((tm, y.shape[1]), lambda i: (i, 0)) for y in ys]
    in_specs += [_resident(wo.shape) for wo in wos]
    in_specs += [_resident((1, d)), _resident(w1.shape), _resident(w2.shape)]
    return pl.pallas_call(
        functools.partial(_out_mlp_kernel, n_mix=n_mix, tf=tf),
        grid=(m // tm,),
        in_specs=in_specs,
        out_specs=pl.BlockSpec((tm, d), lambda i: (i, 0)),
        out_shape=jax.ShapeDtypeStruct((m, d), F32),
        scratch_shapes=[pltpu.VMEM((tm, d), BF16)],
        compiler_params=_params(("arbitrary",)),
        name="out_mlp",
    )(x2d, *ys, *wos, norm_w, w1, w2)


def _swa_kernel(sink_ref, q_ref, kp_ref, kc_ref, vp_ref, vc_ref, o_ref, bias_ref, s_ref, p_ref, rden_ref):
    n = pl.program_id(1)
    blk = SWA_BLOCK
    rows = lax.broadcasted_iota(jnp.int32, (blk, 2 * blk), 0)
    cols = lax.broadcasted_iota(jnp.int32, (blk, 2 * blk), 1)

    @pl.when(jnp.logical_and(pl.program_id(0) == 0, n == 0))
    def _():
        dist = jnp.abs(rows + blk - cols).astype(F32)
        dchunk = rows // CHUNK + 2 - cols // CHUNK
        valid = jnp.logical_and(dchunk >= 0, dchunk <= SWA_BLOCK // CHUNK)
        for h, slope in enumerate(_alibi_slopes(SWA_HEADS)):
            bias_ref[h * blk:(h + 1) * blk, :] = jnp.where(valid, (-LOG2E * slope) * dist, -jnp.inf)

    rep = SWA_HEADS // SWA_KV_HEADS
    first = jnp.where(jnp.logical_and(n == 0, cols < blk), -jnp.inf, 0.0)
    first = jnp.concatenate([first] * rep, axis=0)
    lane = lax.broadcasted_iota(jnp.int32, (1, LANES), 1)
    lo = lane < HEAD_DIM
    zero = jnp.zeros((), BF16)
    head_row = lax.broadcasted_iota(jnp.int32, (rep * blk, 1), 0) // blk

    v_both = []
    for kh in range(SWA_KV_HEADS):
        c0 = (kh // 2) * LANES
        k128 = jnp.concatenate([kp_ref[0, :, c0:c0 + LANES], kc_ref[0, :, c0:c0 + LANES]], axis=0)
        v128 = jnp.concatenate([vp_ref[0, :, c0:c0 + LANES], vc_ref[0, :, c0:c0 + LANES]], axis=0)
        k_sw = pltpu.roll(k128, HEAD_DIM, 1)
        v_sw = pltpu.roll(v128, HEAD_DIM, 1)
        k_both = jnp.where(lo, k128, k_sw) if kh % 2 == 0 else jnp.where(lo, k_sw, k128)
        v_both.append(jnp.where(lo, v128, v_sw) if kh % 2 == 0 else jnp.where(lo, v_sw, v128))
        parts = []
        for pr in range(rep // 2):
            pair = kh * (rep // 2) + pr
            q128 = q_ref[0, :, pair * LANES:(pair + 1) * LANES]
            parts += [jnp.where(lo, q128, zero), jnp.where(lo, zero, q128)]
        q4 = jnp.concatenate(parts, axis=0)
        s = lax.dot_general(q4, k_both, NT_DIMS, preferred_element_type=F32)
        s_ref[kh] = s + bias_ref[kh * rep * blk:(kh + 1) * rep * blk, :] + first
    for kh in range(SWA_KV_HEADS):
        sink = jnp.zeros((rep * blk, 1), F32)
        for r in range(rep):
            sink = jnp.where(head_row == r, sink_ref[kh * rep + r], sink)
        sink = sink * LOG2E
        s = s_ref[kh]
        m = jnp.maximum(jnp.max(s, axis=-1, keepdims=True), sink)
        ex = jnp.exp2(s - m)
        den = jnp.sum(ex, axis=-1, keepdims=True) + jnp.exp2(sink - m)
        p_ref[kh] = ex.astype(BF16)
        rden_ref[kh] = jnp.broadcast_to(1.0 / den, (rep * blk, LANES))
    for kh in range(SWA_KV_HEADS):
        o4 = jnp.dot(p_ref[kh], v_both[kh], preferred_element_type=F32) * rden_ref[kh]
        for pr in range(rep // 2):
            pair = kh * (rep // 2) + pr
            out = jnp.where(lo, o4[2 * pr * blk:(2 * pr + 1) * blk], o4[(2 * pr + 1) * blk:(2 * pr + 2) * blk])
            o_ref[0, :, pair * LANES:(pair + 1) * LANES] = out.astype(o_ref.dtype)


def _swa(proj3, sinks, q_col, k_col, v_col):
    b, t, _ = proj3.shape
    blk = SWA_BLOCK
    nb = t // blk
    dq = SWA_HEADS * HEAD_DIM
    dkv = SWA_KV_HEADS * HEAD_DIM
    qb, kb, vb = q_col // dq, k_col // dkv, v_col // dkv
    prev = lambda n: jnp.maximum(n - 1, 0)
    return pl.pallas_call(
        _swa_kernel,
        grid=(b, nb),
        in_specs=[
            pl.BlockSpec(memory_space=pltpu.SMEM),
            pl.BlockSpec((1, blk, dq), lambda i, n: (i, n, qb)),
            pl.BlockSpec((1, blk, dkv), lambda i, n: (i, prev(n), kb)),
            pl.BlockSpec((1, blk, dkv), lambda i, n: (i, n, kb)),
            pl.BlockSpec((1, blk, dkv), lambda i, n: (i, prev(n), vb)),
            pl.BlockSpec((1, blk, dkv), lambda i, n: (i, n, vb)),
        ],
        out_specs=pl.BlockSpec((1, blk, dq), lambda i, n: (i, n, 0)),
        out_shape=jax.ShapeDtypeStruct((b, t, dq), BF16),
        scratch_shapes=[
            pltpu.VMEM((SWA_HEADS * blk, 2 * blk), F32),
            pltpu.VMEM((SWA_KV_HEADS, SWA_HEADS // SWA_KV_HEADS * blk, 2 * blk), F32),
            pltpu.VMEM((SWA_KV_HEADS, SWA_HEADS // SWA_KV_HEADS * blk, 2 * blk), BF16),
            pltpu.VMEM((SWA_KV_HEADS, SWA_HEADS // SWA_KV_HEADS * blk, LANES), F32),
        ],
        compiler_params=_params(("arbitrary", "arbitrary")),
        name="swa",
    )(sinks, proj3, proj3, proj3, proj3, proj3)


def _split_bf16(a):
    hi = a.astype(BF16)
    lo = (a - hi.astype(F32)).astype(BF16)
    return hi, lo


def _ssd_kernel(z_ref, xs_ref, bc_ref, xsp_ref, bcp_ref, dt_ref, cw_ref, cb_ref, dtb_ref, alog_ref, dskip_ref,
                nw_ref, o_ref, st_ref):
    tt = SSD_TILE
    inner = SSD_INNER
    gw = inner // SSD_GROUPS
    ns = SSD_STATE
    t = pl.program_id(1)

    half = tt // 2

    @pl.when(t == 0)
    def _():
        st_ref[...] = jnp.zeros_like(st_ref)

    si = lax.broadcasted_iota(jnp.int32, ((SSD_CONV - 1) * half, tt), 0)
    sc = lax.broadcasted_iota(jnp.int32, ((SSD_CONV - 1) * half, tt), 1)
    sel = (sc == (si % half) + half - (SSD_CONV - 1) + si // half).astype(BF16)

    def conv_silu(cur_ref, prev_ref, c0, c1):
        outs = []
        for hh in range(2):
            cur = cur_ref[0, hh * half:(hh + 1) * half, :]
            if hh == 0:
                before = jnp.where(t > 0, prev_ref[0], jnp.zeros((), BF16))
            else:
                before = cur_ref[0, 0:half, :]
            sh = jnp.dot(sel, jnp.concatenate([before, cur], axis=0), preferred_element_type=F32)
            conv = cb_ref[:, c0:c1] + cw_ref[SSD_CONV - 1:SSD_CONV, c0:c1] * cur.astype(F32)
            for k in range(SSD_CONV - 1):
                conv = conv + cw_ref[k:k + 1, c0:c1] * sh[k * half:(k + 1) * half]
            outs.append(conv * jax.nn.sigmoid(conv))
        return jnp.concatenate(outs, axis=0)

    xs = conv_silu(xs_ref, xsp_ref, 0, inner)
    bcm = conv_silu(bc_ref, bcp_ref, inner, inner + 2 * SSD_GROUPS * ns).astype(BF16)
    bm = bcm[:, 0:SSD_GROUPS * ns]
    cm = bcm[:, SSD_GROUPS * ns:]

    dtx = dt_ref[0] + dtb_ref[...]
    dt_c = jnp.maximum(dtx, 0.0) + jnp.log(1.0 + jnp.exp(-jnp.abs(dtx)))
    adt_c = dt_c * (-jnp.exp(alog_ref[...]))
    er = lax.broadcasted_iota(jnp.int32, (LANES, inner), 0)
    ec = lax.broadcasted_iota(jnp.int32, (LANES, inner), 1)
    expand = (er == ec // HEAD_DIM).astype(BF16)
    d_hi, d_lo = _split_bf16(dt_c)
    dt = jnp.dot(d_hi, expand, preferred_element_type=F32) + jnp.dot(d_lo, expand, preferred_element_type=F32)
    a_hi, a_lo = _split_bf16(adt_c)
    a_hi = jnp.dot(a_hi, expand, preferred_element_type=F32).astype(BF16)
    a_lo = jnp.dot(a_lo, expand, preferred_element_type=F32).astype(BF16)

    ri = lax.broadcasted_iota(jnp.int32, (tt, tt), 0)
    ci = lax.broadcasted_iota(jnp.int32, (tt, tt), 1)
    tril = jnp.logical_and(ri // CHUNK == ci // CHUNK, ci <= ri).astype(BF16)
    rl = lax.broadcasted_iota(jnp.int32, (tt, 1), 0) % CHUNK
    cl = lax.broadcasted_iota(jnp.int32, (1, inner), 1) % HEAD_DIM

    a_cum = jnp.dot(tril, a_hi, preferred_element_type=F32) + jnp.dot(tril, a_lo, preferred_element_type=F32)
    later = rl > cl
    zero_b = jnp.zeros((), BF16)
    seg = (jnp.dot(tril, jnp.where(later, a_hi, zero_b), preferred_element_type=F32)
           + jnp.dot(tril, jnp.where(later, a_lo, zero_b), preferred_element_type=F32))
    decay = jnp.exp(jnp.where(cl <= rl, seg, -jnp.inf))

    xdt = xs * dt
    quad = lax.broadcasted_iota(jnp.int32, (CHUNK, 4 * HEAD_DIM), 1) // HEAD_DIM
    zero = jnp.zeros((), BF16)

    for c in range(tt // CHUNK):
        r0 = c * CHUNK
        rs = slice(r0, r0 + CHUNK)
        acum_c = a_cum[rs]
        a_last = a_cum[r0 + CHUNK - 1:r0 + CHUNK]
        xdt_c = xdt[rs]
        xdt_b = xdt_c.astype(BF16)
        cbs = []
        for g in range(SSD_GROUPS):
            b_g = bm[rs, g * ns:(g + 1) * ns]
            c_g = cm[rs, g * ns:(g + 1) * ns]
            b_tiled = jnp.concatenate([b_g] * (gw // HEAD_DIM), axis=0)
            cbs.append(lax.dot_general(c_g, b_tiled, NT_DIMS, preferred_element_type=F32))
        gmat = (jnp.concatenate(cbs, axis=1) * decay[rs]).astype(BF16)
        yd = []
        for qd in range(inner // (4 * HEAD_DIM)):
            cs = slice(qd * 4 * HEAD_DIM, (qd + 1) * 4 * HEAD_DIM)
            xq = xdt_b[:, cs]
            xbd = jnp.concatenate([jnp.where(quad == jj, xq, zero) for jj in range(4)], axis=0)
            yd.append(jnp.dot(gmat[:, cs], xbd, preferred_element_type=F32))
        y = jnp.concatenate(yd, axis=1)
        din = jnp.exp(acum_c)
        x_end = (xdt_c * jnp.exp(a_last - acum_c)).astype(BF16)
        cdec = jnp.exp(a_last)
        yo = []
        for g in range(SSD_GROUPS):
            gs = slice(g * gw, (g + 1) * gw)
            b_g = bm[rs, g * ns:(g + 1) * ns]
            c_g = cm[rs, g * ns:(g + 1) * ns]
            st_g = st_ref[:, gs]
            yo.append(jnp.dot(c_g, st_g.astype(BF16), preferred_element_type=F32))
            upd = lax.dot_general(b_g, x_end[:, gs], TN_DIMS, preferred_element_type=F32)
            st_ref[:, gs] = st_g * cdec[:, gs] + upd
        y = y + jnp.concatenate(yo, axis=1) * din
        y = y + xs[rs] * dskip_ref[...]
        zz = z_ref[0, rs, :].astype(F32)
        y = y * (zz * jax.nn.sigmoid(zz))
        outs = []
        for g in range(SSD_GROUPS):
            y_g = y[:, g * gw:(g + 1) * gw]
            ms = jnp.mean(y_g * y_g, axis=-1, keepdims=True)
            outs.append(y_g * lax.rsqrt(ms + EPS))
        o_ref[0, rs, :] = (jnp.concatenate(outs, axis=1) * nw_ref[...]).astype(o_ref.dtype)


def _ssd(proj3, dtx3, conv_w, conv_b, dtb, alog, dskip, nw, z_col, xs_col, bc_col):
    b, t, _ = proj3.shape
    tt = SSD_TILE
    assert t % tt == 0
    inner = SSD_INNER
    bcw = 2 * SSD_GROUPS * SSD_STATE
    cdim = inner + bcw
    full = lambda shape: pl.BlockSpec(shape, lambda i, s: (0,) * len(shape))
    return pl.pallas_call(
        _ssd_kernel,
        grid=(b, t // tt),
        in_specs=[
            pl.BlockSpec((1, tt, inner), lambda i, s: (i, s, z_col // inner)),
            pl.BlockSpec((1, tt, inner), lambda i, s: (i, s, xs_col // inner)),
            pl.BlockSpec((1, tt, bcw), lambda i, s: (i, s, bc_col // bcw)),
            pl.BlockSpec((1, tt // 2, inner), lambda i, s: (i, jnp.maximum(2 * s - 1, 0), xs_col // inner)),
            pl.BlockSpec((1, tt // 2, bcw), lambda i, s: (i, jnp.maximum(2 * s - 1, 0), bc_col // bcw)),
            pl.BlockSpec((1, tt, LANES), lambda i, s: (i, s, 0)),
            full((SSD_CONV, cdim)), full((1, cdim)), full((1, LANES)), full((1, LANES)),
            full((1, inner)), full((1, inner)),
        ],
        out_specs=pl.BlockSpec((1, tt, inner), lambda i, s: (i, s, 0)),
        out_shape=jax.ShapeDtypeStruct((b, t, inner), BF16),
        scratch_shapes=[pltpu.VMEM((SSD_STATE, inner), F32)],
        compiler_params=_params(("arbitrary", "arbitrary")),
        name="ssd",
    )(proj3, proj3, proj3, proj3, proj3, dtx3, conv_w, conv_b, dtb, alog, dskip, nw)


def _round_to_bf16(x):
    bits = struct.unpack("<I", struct.pack("<f", x))[0]
    bits = (bits + 0x7FFF + ((bits >> 16) & 1)) & 0xFFFF0000
    return struct.unpack("<f", struct.pack("<I", bits))[0]


def _bf16_terms(x, n):
    out = []
    for _ in range(n):
        out.append(_round_to_bf16(x))
        x -= out[-1]
    return out


LOG2E = math.log2(math.e)
LOG2E_TERMS = _bf16_terms(LOG2E, 3)
DIFF_BIAS_SPLIT = 16


def _diff_attn_kernel(slope_ref, lam_ref, q_ref, k_ref, v_ref, sn_ref, o_ref, tdiag_ref, kaug_ref, vt_ref,
                      *scratch, lam_init, nq):
    blk = DIFF_BLOCK
    strip = DIFF_STRIP
    ns = DIFF_SCORE_SLOTS
    qm_ref, p_ref, m_ref, l_ref, acc_ref, lfin_ref, accfin_ref = (scratch[2 * i:2 * i + 2] for i in range(7))
    s_ref = [scratch[14 + 2 * i:16 + 2 * i] for i in range(ns)]
    slope = jnp.full((1, 1), slope_ref[pl.program_id(1)], F32)
    kk = lax.broadcasted_iota(jnp.int32, (blk, blk), 0)
    qq = lax.broadcasted_iota(jnp.int32, (blk, blk), 1)
    tdiag_ref[...] = jnp.where(kk // CHUNK > qq // CHUNK, -jnp.inf,
                               jnp.where(kk > qq, (2.0 * LOG2E) * slope * (qq - kk).astype(F32), 0.0))

    lam_p = lam_ref[...]
    lam = (jnp.exp(jnp.sum(lam_p[0:1] * lam_p[1:2], axis=-1, keepdims=True))
           - jnp.exp(jnp.sum(lam_p[2:3] * lam_p[3:4], axis=-1, keepdims=True)) + lam_init)

    lane = lax.broadcasted_iota(jnp.int32, (blk, LANES), 1)
    pos = lax.broadcasted_iota(jnp.int32, (blk, LANES), 0)
    lo = lane < HEAD_DIM
    nt = len(LOG2E_TERMS)
    b_hi = slope * (DIFF_BIAS_SPLIT * (pos // DIFF_BIAS_SPLIT)).astype(F32)
    b_lo = slope * (pos % DIFF_BIAS_SPLIT).astype(F32)
    kbias, qconst = [], []
    for base in (HEAD_DIM, 0):
        rel = lane - base
        kbias.append(jnp.where((rel >= 0) & (rel < nt), b_hi,
                               jnp.where((rel >= nt) & (rel < 2 * nt), b_lo, 0.0)).astype(BF16))
        qc = jnp.zeros((blk, LANES), F32)
        for i, term in enumerate(LOG2E_TERMS):
            qc = jnp.where((rel == i) | (rel == nt + i), term, qc)
        qconst.append(qc.astype(BF16))
    for j in range(nq):
        k = k_ref[0, j * blk:(j + 1) * blk, :]
        kaug_ref[0, j] = jnp.where(lo, k, kbias[0])
        kaug_ref[1, j] = jnp.where(lo, kbias[1], k)
        vt_ref[j] = v_ref[0, j * blk:(j + 1) * blk, :].astype(F32).T.astype(BF16)

    def scores(slot, br, j, diag):
        s = lax.dot_general(kaug_ref[br, j], qm_ref[br][...], NT_DIMS, preferred_element_type=F32)
        if diag:
            s = s + tdiag_ref[...]
        s_ref[slot][br][...] = s
        return jnp.max(s, axis=0, keepdims=True)

    def exp_strips(br, src, off):
        lp = [None, None]
        for i, r in enumerate(range(0, blk, strip)):
            p = jnp.exp2(src[r:r + strip, :] - off)
            part = p[0:8]
            for g in range(8, strip, 8):
                part = part + p[g:g + 8]
            lp[i % 2] = part if lp[i % 2] is None else lp[i % 2] + part
            p_ref[br][r:r + strip, :] = p.astype(BF16)
        return lp[0] + lp[1]

    def absorb(slot, br, j, mx, shift):
        m_old = m_ref[br][0:1, :]
        m_new = jnp.maximum(m_old, mx + shift)
        alpha = jnp.exp2(m_old - m_new)
        m_ref[br][...] = jnp.broadcast_to(m_new, m_ref[br].shape)
        lp = exp_strips(br, s_ref[slot][br], m_new - shift)
        l_ref[br][...] = alpha * l_ref[br][...] + lp
        pv = jnp.dot(vt_ref[j], p_ref[br][...], preferred_element_type=F32)
        acc_ref[br][...] = alpha * acc_ref[br][...] + pv

    def raw_scores(br, j):
        return lax.dot_general(kaug_ref[br, j], qm_ref[br][...], NT_DIMS, preferred_element_type=F32)

    def absorb_streaming(br, j, s, shift):
        m_old = m_ref[br][0:1, :]
        excess = jnp.max(s, axis=0, keepdims=True) + shift - m_old
        lp = exp_strips(br, s, m_old - shift)
        pv = jnp.dot(vt_ref[j], p_ref[br][...], preferred_element_type=F32)
        up = jnp.maximum(excess, 0.0)
        alpha = jnp.exp2(-up)
        m_ref[br][...] = jnp.broadcast_to(m_old + up, m_ref[br].shape)
        l_ref[br][...] = alpha * (l_ref[br][...] + lp)
        acc_ref[br][...] = alpha * (acc_ref[br][...] + pv)
        return excess

    zero_shift = jnp.zeros((1, 1), F32)

    def shift_of(qi, j):
        return (-LOG2E * blk) * slope * jnp.full((1, 1), qi - j, jnp.int32).astype(F32)

    def reset(br, m_init):
        m_ref[br][...] = jnp.broadcast_to(m_init, m_ref[br].shape)
        l_ref[br][...] = jnp.zeros(l_ref[br].shape, F32)
        acc_ref[br][...] = jnp.zeros(acc_ref[br].shape, F32)

    def load_queries(qi):
        q = q_ref[0, pl.ds(pl.multiple_of(qi * blk, blk), blk), :]
        qm_ref[0][...] = jnp.where(lo, q, qconst[0])
        qm_ref[1][...] = jnp.where(lo, qconst[1], q)

    def write_output(qi, l_src, acc_src):
        r1 = 1.0 / jnp.sum(l_src[0][...], axis=0, keepdims=True)
        r2 = lam / jnp.sum(l_src[1][...], axis=0, keepdims=True)
        o = (acc_src[0][...] * r1 - acc_src[1][...] * r2).T
        ms = jnp.mean(o * o, axis=-1, keepdims=True)
        o = o * lax.rsqrt(ms + EPS) * sn_ref[...] * (1.0 - lam_init)
        q0 = qi * blk if isinstance(qi, int) else pl.multiple_of(qi * blk, blk)
        o_ref[0, pl.ds(q0, blk), :] = o.astype(o_ref.dtype)

    for br in range(2):
        lfin_ref[br][...] = jnp.ones(lfin_ref[br].shape, F32)
        accfin_ref[br][...] = jnp.zeros(accfin_ref[br].shape, F32)

    def q_step(qi, worst):
        write_output(jnp.maximum(qi - 1, 0), lfin_ref, accfin_ref)
        load_queries(qi)
        s = []
        for br in range(2):
            prod = (kaug_ref[br, qi].astype(F32) * qm_ref[br][...].astype(F32)).astype(BF16)
            own = lax.dot_general(jnp.ones((8, LANES), BF16), prod, NT_DIMS, preferred_element_type=F32)
            reset(br, own[0:1, :])
            s.append(raw_scores(br, qi) + tdiag_ref[...])
        worst = jnp.maximum(worst, jnp.maximum(absorb_streaming(0, qi, s[0], zero_shift),
                                               absorb_streaming(1, qi, s[1], zero_shift)))

        nu = DIFF_BLOCKS_PER_TRIP

        def run_blocks(first, count, w):
            s_next = [raw_scores(br, first) for br in range(2)] if count else None
            for i in range(count):
                j = first + i
                s = s_next
                if i + 1 < count:
                    s_next = [raw_scores(br, j + 1) for br in range(2)]
                shift = shift_of(qi, j)
                e0 = absorb_streaming(0, j, s[0], shift)
                e1 = absorb_streaming(1, j, s[1], shift)
                w = jnp.maximum(w, jnp.maximum(e0, e1))
            return w

        worst = lax.fori_loop(0, qi // nu, lambda t, w: run_blocks(nu * t, nu, w), worst)
        rem = qi % nu
        worst = lax.switch(rem, [functools.partial(run_blocks, qi - rem, r) for r in range(nu)], worst)
        for br in range(2):
            lfin_ref[br][...] = l_ref[br][...]
            accfin_ref[br][...] = acc_ref[br][...]
        return worst

    worst = lax.fori_loop(0, nq, q_step, jnp.full((1, blk), -jnp.inf, F32))
    write_output(nq - 1, lfin_ref, accfin_ref)

    @pl.when(jnp.max(worst) > DIFF_STREAM_HEADROOM)
    def _():
        def q_step_two_pass(qi, _):
            load_queries(qi)
            mx = [scores(0, br, qi, True) for br in range(2)]
            for br in range(2):
                reset(br, -jnp.inf)
                absorb(0, br, qi, mx[br], zero_shift)

            def kv_step(j, _):
                mx = [scores(0, br, j, False) for br in range(2)]
                for br in range(2):
                    absorb(0, br, j, mx[br], shift_of(qi, j))
                return 0

            lax.fori_loop(0, qi, kv_step, 0)
            write_output(qi, l_ref, acc_ref)
            return 0

        lax.fori_loop(0, nq, q_step_two_pass, 0)


def _diff_attn(proj3, lam_p, sub_norm, lam_init):
    b, t, _ = proj3.shape
    blk = DIFF_BLOCK
    assert t % blk == 0
    hd = 2 * HEAD_DIM
    nh = DIFF_HEADS
    assert all(math.log2(s).is_integer() for s in _alibi_slopes(nh))
    return pl.pallas_call(
        functools.partial(_diff_attn_kernel, lam_init=lam_init, nq=t // blk),
        grid=(b, nh),
        in_specs=[
            pl.BlockSpec(memory_space=pltpu.SMEM),
            pl.BlockSpec((4, HEAD_DIM), lambda i, h: (0, 0)),
            pl.BlockSpec((1, t, hd), lambda i, h: (i, 0, h)),
            pl.BlockSpec((1, t, hd), lambda i, h: (i, 0, nh + h)),
            pl.BlockSpec((1, t, hd), lambda i, h: (i, 0, 2 * nh + h)),
            pl.BlockSpec((1, hd), lambda i, h: (0, 0)),
        ],
        out_specs=pl.BlockSpec((1, t, hd), lambda i, h: (i, 0, h)),
        out_shape=jax.ShapeDtypeStruct((b, t, nh * hd), BF16),
        scratch_shapes=[
            pltpu.VMEM((blk, blk), F32), pltpu.VMEM((2, t // blk, blk, hd), BF16),
            pltpu.VMEM((t // blk, hd, blk), BF16),
            *[pltpu.VMEM((blk, hd), BF16)] * 2, *[pltpu.VMEM((blk, blk), BF16)] * 2,
            *[pltpu.VMEM((8, blk), F32)] * 2, *[pltpu.VMEM((8, blk), F32)] * 2, *[pltpu.VMEM((hd, blk), F32)] * 2,
            *[pltpu.VMEM((8, blk), F32)] * 2, *[pltpu.VMEM((hd, blk), F32)] * 2,
            *[pltpu.VMEM((blk, blk), F32)] * (2 * DIFF_SCORE_SLOTS),
        ],
        compiler_params=_params(("arbitrary", "arbitrary")),
        name="diff_attn",
    )(jnp.asarray(_alibi_slopes(nh), F32), lam_p, proj3, proj3, proj3, sub_norm)


def _group_sum_matrix():
    idx = jnp.arange(GROUP_TILE) // HEAD_DIM
    return (idx[:, None] == idx[None, :]).astype(BF16)


def kernel(x, ev_norm_w, ev_w_in, ev_conv_w, ev_conv_b, ev_dt_bias, ev_a_log, ev_d_skip, ev_ssd_norm_w,
           ev_q_norm, ev_k_norm, ev_sinks, ev_w_out,
           od_norm_w, od_w_in, od_q_norm, od_k_norm, od_lam_q1, od_lam_k1, od_lam_q2, od_lam_k2,
           od_sub_norm, od_w_out, mlp_norm_w, mlp_w1, mlp_w2):
    bsz, t, d = x.shape
    m = bsz * t
    depth = mlp_w1.shape[0]
    gmat = _group_sum_matrix()
    x2 = x.reshape(m, d)
    for layer in range(depth):
        j = layer // 2
        if layer % 2 == 0:
            w_in = ev_w_in[j]
            inner = SSD_INNER
            bcw = 2 * SSD_GROUPS * SSD_STATE
            dq = SWA_HEADS * HEAD_DIM
            dkv = SWA_KV_HEADS * HEAD_DIM
            o_xbc = inner
            o_dt = o_xbc + inner + bcw
            o_q = o_dt + SSD_HEADS
            o_k = o_q + dq
            o_v = o_k + dkv
            w_re = jnp.concatenate([
                w_in[:, 0:inner], w_in[:, o_xbc:o_xbc + inner], w_in[:, o_q:o_q + dq],
                w_in[:, o_xbc + inner:o_dt], w_in[:, o_k:o_k + dkv], w_in[:, o_v:o_v + dkv]], axis=1).astype(BF16)
            z_col, xs_col, q_col, bc_col = 0, inner, 2 * inner, 2 * inner + dq
            k_col = bc_col + bcw
            v_col = k_col + dkv
            n_re = v_col + dkv
            pad_heads = lambda a: jnp.pad(a, [(0, 0)] * (a.ndim - 1) + [(0, LANES - SSD_HEADS)])
            w_dt = pad_heads(w_in[:, o_dt:o_dt + SSD_HEADS]).astype(BF16)
            ones = lambda n: jnp.ones((n,), F32)
            gw = jnp.concatenate([ones(q_col), jnp.tile(ev_q_norm[j], SWA_HEADS) * (LOG2E * HEAD_DIM ** -0.5), ones(bcw),
                                  jnp.tile(ev_k_norm[j], SWA_KV_HEADS), ones(dkv)]).reshape(1, n_re)
            nm = jnp.concatenate([jnp.zeros((q_col,), F32), ones(dq), jnp.zeros((bcw,), F32), ones(dkv),
                                  jnp.zeros((dkv,), F32)]).reshape(1, n_re)
            norm_tiles = sorted({c // GROUP_TILE for c in list(range(q_col, q_col + dq, HEAD_DIM))
                                 + list(range(k_col, k_col + dkv, HEAD_DIM))})
            proj, dtx = _rms_proj(x2, ev_norm_w[j].reshape(1, d), w_re, gw, nm, gmat, norm_tiles, wd=w_dt)
            proj3 = proj.reshape(bsz, t, n_re)
            rep = lambda v: jnp.repeat(v, HEAD_DIM).reshape(1, inner)
            y_ssd = _ssd(proj3, dtx.reshape(bsz, t, LANES), ev_conv_w[j], ev_conv_b[j].reshape(1, -1),
                         pad_heads(ev_dt_bias[j]).reshape(1, LANES), pad_heads(ev_a_log[j]).reshape(1, LANES),
                         rep(ev_d_skip[j]),
                         ev_ssd_norm_w[j].reshape(1, inner), z_col, xs_col, bc_col)
            y_swa = _swa(proj3, ev_sinks[j], q_col, k_col, v_col)
            w_out = ev_w_out[j].astype(BF16)
            ys = [y_ssd.reshape(m, inner), y_swa.reshape(m, dq)]
            wos = [w_out[0:inner], w_out[inner:]]
        else:
            w_in = od_w_in[j].astype(BF16)
            dm = 2 * DIFF_HEADS * HEAD_DIM
            ones = jnp.ones((dm,), F32)
            gw = jnp.concatenate([jnp.tile(od_q_norm[j], 2 * DIFF_HEADS) * (LOG2E * HEAD_DIM ** -0.5),
                                  jnp.tile(od_k_norm[j], 2 * DIFF_HEADS), ones]).reshape(1, 3 * dm)
            nm = jnp.concatenate([ones, ones, jnp.zeros((dm,), F32)]).reshape(1, 3 * dm)
            norm_tiles = list(range(2 * dm // GROUP_TILE))
            (proj,) = _rms_proj(x2, od_norm_w[j].reshape(1, d), w_in, gw, nm, gmat, norm_tiles)
            lam_init = 0.8 - 0.6 * math.exp(-0.3 * layer)
            lam_p = jnp.stack([od_lam_q1[j], od_lam_k1[j], od_lam_q2[j], od_lam_k2[j]])
            y = _diff_attn(proj.reshape(bsz, t, 3 * dm), lam_p, od_sub_norm[j].reshape(1, -1), lam_init)
            ys = [y.reshape(m, dm)]
            wos = [od_w_out[j].astype(BF16)]
        x2 = _out_mlp(x2, ys, wos, mlp_norm_w[layer].reshape(1, d), mlp_w1[layer].astype(BF16),
                      mlp_w2[layer].astype(BF16))
    return x2.reshape(bsz, t, d)
```
